```python
import jax, jax.numpy as jnp
from jax import lax
import numpy as np

D_MODEL = 1024
BATCH = 16
SEQ = 256
DEPTH = 4
DEC_BATCH = 4
DEC_SEQ = 4096
PAST_LEN = 512

GRID_W = 64
Q_BLOCK = 128
A_HEADS = 8
A_KV_HEADS = 2
A_HEAD_DIM = 64
A_GROUP = A_HEADS // A_KV_HEADS
A_WINDOW = 128
B_HEADS = 8
B_Q_LORA = 384
B_KV_LORA = 256
B_NOPE = 64
B_ROPE = 32
B_QK_DIM = B_NOPE + B_ROPE
B_V_DIM = 64
C_WIDTH = 512
C_BLOCKS = 8
C_BLOCK_W = C_WIDTH // C_BLOCKS
C_CONV = 4
C_RG = 8.0
MIX_W = 512
N_BRANCH = 3
D_FF = 4 * D_MODEL
N_ADA = 6
ROPE_BASE = 10000.0
EPS = 1e-6
NEG_INF = -1e30
IN_SPLITS = (A_HEADS * A_HEAD_DIM, A_KV_HEADS * A_HEAD_DIM, A_KV_HEADS * A_HEAD_DIM,
             B_Q_LORA, B_KV_LORA, B_ROPE, C_WIDTH, C_WIDTH, N_BRANCH * D_MODEL)
IN_COLS = sum(IN_SPLITS)

kernel_name = 'hybrid_prefix_diffusion_step'

F32 = jnp.float32


def rms_norm(x, g):
    xf = x.astype(F32)
    y = xf * lax.rsqrt(jnp.mean(xf * xf, axis=-1, keepdims=True) + EPS)
    return (y * g.astype(F32)).astype(x.dtype)


def split_cols(proj):
    outs = []
    off = 0
    for n in IN_SPLITS:
        outs.append(proj[..., off:off + n])
        off += n
    return outs


def ada(cvec, lp):
    mod = jax.nn.silu(cvec) @ lp['w_ada'] + lp['b_ada']
    return jnp.split(mod, N_ADA, axis=-1)


def rope_1d(x, pos):
    m = x.shape[-1]
    half = m // 2
    inv = jnp.power(ROPE_BASE, -jnp.arange(half, dtype=F32) * (2.0 / m))
    ang = pos.astype(F32)[:, None] * inv
    cos = jnp.cos(ang)[:, None, :]
    sin = jnp.sin(ang)[:, None, :]
    xf = x.astype(F32)
    x1, x2 = xf[..., :half], xf[..., half:]
    return jnp.concatenate([x1 * cos - x2 * sin, x2 * cos + x1 * sin], axis=-1).astype(x.dtype)


def axial_rope(x, rows, cols):
    half = x.shape[-1] // 2
    return jnp.concatenate([rope_1d(x[..., :half], rows), rope_1d(x[..., half:], cols)], axis=-1)


def map_query_blocks(fn, q):
    b, t = q.shape[0], q.shape[1]
    nb = t // Q_BLOCK
    qb = jnp.moveaxis(q.reshape((b, nb, Q_BLOCK) + q.shape[2:]), 1, 0)
    out = lax.map(fn, (jnp.arange(nb), qb))
    out = jnp.moveaxis(out, 0, 1)
    return out.reshape((b, t) + out.shape[3:])


def sink_softmax_av(s, sink, v):
    sk = sink.astype(F32).reshape(1, A_KV_HEADS, A_GROUP, 1, 1)
    m = jnp.maximum(jnp.max(s, axis=-1, keepdims=True), sk)
    p = jnp.exp(s - m)
    p = p / (jnp.sum(p, axis=-1, keepdims=True) + jnp.exp(sk - m))
    o = jnp.einsum('bkgqs,bskd->bqkgd', p.astype(v.dtype), v)
    return o.reshape(o.shape[0], o.shape[1], A_HEADS * A_HEAD_DIM)


def gqa_context(q, k, v, sink):
    scale = A_HEAD_DIM ** -0.5
    def block(args):
        _, qb = args
        qg = qb.reshape(qb.shape[0], Q_BLOCK, A_KV_HEADS, A_GROUP, A_HEAD_DIM)
        s = jnp.einsum('bqkgd,bskd->bkgqs', qg, k).astype(F32) * scale
        return sink_softmax_av(s, sink, v)
    return map_query_blocks(block, q)


def gqa_window_latent(q, k, v, ctx_k, ctx_v, sink):
    n = q.shape[1]
    pad = ((0, 0), (Q_BLOCK, Q_BLOCK), (0, 0), (0, 0))
    kp, vp = jnp.pad(k, pad), jnp.pad(v, pad)
    scale = A_HEAD_DIM ** -0.5
    def block(args):
        i, qb = args
        start = i * Q_BLOCK
        kw = lax.dynamic_slice_in_dim(kp, start, 3 * Q_BLOCK, axis=1)
        vw = lax.dynamic_slice_in_dim(vp, start, 3 * Q_BLOCK, axis=1)
        qpos = start + jnp.arange(Q_BLOCK)
        kpos = start - Q_BLOCK + jnp.arange(3 * Q_BLOCK)
        valid = ((kpos >= 0) & (kpos < n))[None, :] & (jnp.abs(qpos[:, None] - kpos[None, :]) <= A_WINDOW)
        qg = qb.reshape(qb.shape[0], Q_BLOCK, A_KV_HEADS, A_GROUP, A_HEAD_DIM)
        s_loc = jnp.where(valid, jnp.einsum('bqkgd,bskd->bkgqs', qg, kw).astype(F32) * scale, NEG_INF)
        s_ctx = jnp.einsum('bqkgd,bskd->bkgqs', qg, ctx_k).astype(F32) * scale
        s = jnp.concatenate([s_loc, s_ctx], axis=-1)
        return sink_softmax_av(s, sink, jnp.concatenate([vw, ctx_v], axis=1))
    return map_query_blocks(block, q)


def mla_query(cq, lp):
    b, t = cq.shape[0], cq.shape[1]
    q = rms_norm(cq, lp['g_q']) @ lp['w_q_up']
    return q.reshape(b, t, B_HEADS, B_QK_DIM)


def mla_kv_up(ckv_n, lp):
    b, t = ckv_n.shape[0], ckv_n.shape[1]
    k_nope = (ckv_n @ lp['w_uk']).reshape(b, t, B_HEADS, B_NOPE)
    v = (ckv_n @ lp['w_uv']).reshape(b, t, B_HEADS, B_V_DIM)
    return k_nope, v


def mla_attend(q, k_nope, k_rope, v):
    scale = B_QK_DIM ** -0.5
    def block(args):
        _, qb = args
        qn, qr = qb[..., :B_NOPE], qb[..., B_NOPE:]
        s = (jnp.einsum('bqhd,bshd->bhqs', qn, k_nope).astype(F32)
             + jnp.einsum('bqhr,bsr->bhqs', qr, k_rope).astype(F32)) * scale
        p = jax.nn.softmax(s, axis=-1)
        o = jnp.einsum('bhqs,bshd->bqhd', p.astype(v.dtype), v)
        return o.reshape(o.shape[0], Q_BLOCK, B_HEADS * B_V_DIM)
    return map_query_blocks(block, q)


def centred_dwconv(x, w, bias):
    t = x.shape[1]
    xp = jnp.pad(x, ((0, 0), (1, C_CONV - 2), (0, 0)))
    out = bias
    for j in range(C_CONV):
        out = out + xp[:, j:j + t] * w[j]
    return out


def rglru(x, lam, w_a, b_a, w_i, b_i, h0):
    b, t, _ = x.shape
    xf = x.astype(F32)
    xb = xf.reshape(b, t, C_BLOCKS, C_BLOCK_W)
    r = jax.nn.sigmoid(jnp.einsum('btnc,ncd->btnd', xb, w_a.astype(F32)).reshape(b, t, C_WIDTH) + b_a.astype(F32))
    ig = jax.nn.sigmoid(jnp.einsum('btnc,ncd->btnd', xb, w_i.astype(F32)).reshape(b, t, C_WIDTH) + b_i.astype(F32))
    log_a = -C_RG * r * jax.nn.softplus(-lam.astype(F32))
    a = jnp.exp(log_a)
    u = jnp.sqrt(-jnp.expm1(2.0 * log_a)) * (ig * xf)
    u = u.at[:, 0].add(a[:, 0] * h0.astype(F32))
    def combine(e1, e2):
        a1, b1 = e1
        a2, b2 = e2
        return a1 * a2, a2 * b1 + b2
    _, h = lax.associative_scan(combine, (a, u), axis=1)
    return h


def rglru_bidir(x, lp, h0_fwd, h0_bwd):
    hf = rglru(x, lp['lam'][0], lp['w_ra'][0], lp['b_ra'][0], lp['w_ri'][0], lp['b_ri'][0], h0_fwd)
    hb = jnp.flip(rglru(jnp.flip(x, axis=1), lp['lam'][1], lp['w_ra'][1], lp['b_ra'][1],
                        lp['w_ri'][1], lp['b_ri'][1], h0_bwd), axis=1)
    return hf, hb


def merge_branches(ya, yb, yc, gl, lp):
    ys = jnp.stack([ya, yb, yc], axis=-2)
    proj = jnp.einsum('btnm,nmd->btnd', ys, lp['w_branch'])
    gates = jax.nn.sigmoid(gl.reshape(gl.shape[:-1] + (N_BRANCH, D_MODEL)))
    return jnp.sum(gates * proj, axis=-2) @ lp['w_out']


def mlp_sublayer(x, shift, scale, gate, lp):
    h = rms_norm(x, lp['g_pre2']) * (1 + scale) + shift
    f = jnp.square(jax.nn.relu(h @ lp['w_ff1'])) @ lp['w_ff2']
    return x + gate * rms_norm(f, lp['g_post2'])


def context_layer(x, c_ctx, lp):
    sh1, sc1, g1, sh2, sc2, g2 = ada(c_ctx[None, None, :], lp)
    b, s, _ = x.shape
    h = rms_norm(x, lp['g_pre1']) * (1 + sc1) + sh1
    qa, ka, va, cq, ckv, kr, xc, gc, gl = split_cols(h @ lp['w_in'])
    qa = qa.reshape(b, s, A_HEADS, A_HEAD_DIM)
    ka = ka.reshape(b, s, A_KV_HEADS, A_HEAD_DIM)
    va = va.reshape(b, s, A_KV_HEADS, A_HEAD_DIM)
    ya = gqa_context(qa, ka, va, lp['sink'])
    ckv_n = rms_norm(ckv, lp['g_kv'])
    kn, vb = mla_kv_up(ckv_n, lp)
    yb = mla_attend(mla_query(cq, lp), kn, kr, vb)
    xconv = centred_dwconv(xc, lp['w_conv'], lp['b_conv'])
    zero = jnp.zeros((b, C_WIDTH), F32)
    hf, hb = rglru_bidir(xconv, lp, zero, zero)
    yc = (jax.nn.gelu(gc.astype(F32)) * (hf + hb)).astype(x.dtype)
    x = x + g1 * rms_norm(merge_branches(ya, yb, yc, gl, lp), lp['g_post1'])
    x = mlp_sublayer(x, sh2, sc2, g2, lp)
    lru_state = jnp.stack([hf[:, -1], hb[:, 0]], axis=1).astype(x.dtype)
    return x, ka, va, ckv_n, kr, lru_state


def latent_layer(x, c, ctx_k, ctx_v, ctx_ckv, ctx_krope, lru0, rows, cols, lp):
    sh1, sc1, g1, sh2, sc2, g2 = ada(c[:, None, :], lp)
    b, n, _ = x.shape
    h = rms_norm(x, lp['g_pre1']) * (1 + sc1) + sh1
    qa, ka, va, cq, ckv, kr, xc, gc, gl = split_cols(h @ lp['w_in'])
    qa = axial_rope(qa.reshape(b, n, A_HEADS, A_HEAD_DIM), rows, cols)
    ka = axial_rope(ka.reshape(b, n, A_KV_HEADS, A_HEAD_DIM), rows, cols)
    va = va.reshape(b, n, A_KV_HEADS, A_HEAD_DIM)
    ya = gqa_window_latent(qa, ka, va, ctx_k, ctx_v, lp['sink'])
    qb = mla_query(cq, lp)
    qb = jnp.concatenate([qb[..., :B_NOPE], axial_rope(qb[..., B_NOPE:], rows, cols)], axis=-1)
    kr_lat = axial_rope(kr[:, :, None, :], rows, cols)[:, :, 0]
    kn_lat, v_lat = mla_kv_up(rms_norm(ckv, lp['g_kv']), lp)
    kn_ctx, v_ctx = mla_kv_up(ctx_ckv, lp)
    yb = mla_attend(qb, jnp.concatenate([kn_lat, kn_ctx], axis=1),
                    jnp.concatenate([kr_lat, ctx_krope], axis=1),
                    jnp.concatenate([v_lat, v_ctx], axis=1))
    xconv = centred_dwconv(xc, lp['w_conv'], lp['b_conv'])
    hf, hb = rglru_bidir(xconv, lp, lru0[:, 0], lru0[:, 1])
    yc = (jax.nn.gelu(gc.astype(F32)) * (hf + hb)).astype(x.dtype)
    x = x + g1 * rms_norm(merge_branches(ya, yb, yc, gl, lp), lp['g_post1'])
    return mlp_sublayer(x, sh2, sc2, g2, lp)


def setup_inputs(seed: int = 0) -> dict:
    key = jax.random.key(seed)
    ks = jax.random.split(key, 40)
    def nrm(k, shape, scale):
        return jax.random.normal(k, shape, F32) * scale
    def gain(k, shape):
        return 1.0 + 0.05 * jax.random.normal(k, shape, F32)
    u = jax.random.uniform(ks[20], (DEPTH, 2, C_WIDTH), F32, minval=0.9, maxval=0.999)
    a_base = u ** (1.0 / C_RG)
    lam = jnp.log(a_base) - jnp.log1p(-a_base)
    return {
        'x_prompt': nrm(ks[0], (BATCH, SEQ, D_MODEL), 1.0),
        'x_sample': nrm(ks[1], (DEC_BATCH, DEC_SEQ, D_MODEL), 1.0),
        'cache_attn_k': nrm(ks[2], (DEC_BATCH, DEPTH, PAST_LEN, A_KV_HEADS, A_HEAD_DIM), 1.0),
        'cache_attn_v': nrm(ks[3], (DEC_BATCH, DEPTH, PAST_LEN, A_KV_HEADS, A_HEAD_DIM), 1.0),
        'cache_mla_ckv': nrm(ks[4], (DEC_BATCH, DEPTH, PAST_LEN, B_KV_LORA), 1.0),
        'cache_mla_krope': nrm(ks[5], (DEC_BATCH, DEPTH, PAST_LEN, B_ROPE), 1.0),
        'state_lru': nrm(ks[6], (DEC_BATCH, DEPTH, 2, C_WIDTH), 0.5),
        'c': nrm(ks[7], (DEC_BATCH, D_MODEL), 1.0),
        'c_ctx': nrm(ks[8], (D_MODEL,), 1.0),
        'w_ada': nrm(ks[9], (DEPTH, D_MODEL, N_ADA * D_MODEL), D_MODEL ** -0.5),
        'b_ada': nrm(ks[10], (DEPTH, N_ADA * D_MODEL), 0.02),
        'g_pre1': gain(ks[11], (DEPTH, D_MODEL)),
        'g_post1': gain(ks[12], (DEPTH, D_MODEL)),
        'g_pre2': gain(ks[13], (DEPTH, D_MODEL)),
        'g_post2': gain(ks[14], (DEPTH, D_MODEL)),
        'w_in': nrm(ks[15], (DEPTH, D_MODEL, IN_COLS), D_MODEL ** -0.5),
        'sink': nrm(ks[16], (DEPTH, A_HEADS), 0.5),
        'g_q': gain(ks[17], (DEPTH, B_Q_LORA)),
        'w_q_up': nrm(ks[18], (DEPTH, B_Q_LORA, B_HEADS * B_QK_DIM), B_Q_LORA ** -0.5),
        'g_kv': gain(ks[19], (DEPTH, B_KV_LORA)),
        'w_uk': nrm(ks[21], (DEPTH, B_KV_LORA, B_HEADS * B_NOPE), B_KV_LORA ** -0.5),
        'w_uv': nrm(ks[22], (DEPTH, B_KV_LORA, B_HEADS * B_V_DIM), B_KV_LORA ** -0.5),
        'w_conv': nrm(ks[23], (DEPTH, C_CONV, C_WIDTH), C_CONV ** -0.5),
        'b_conv': nrm(ks[24], (DEPTH, C_WIDTH), 0.02),
        'lam': lam,
        'w_ra': nrm(ks[25], (DEPTH, 2, C_BLOCKS, C_BLOCK_W, C_BLOCK_W), C_BLOCK_W ** -0.5),
        'b_ra': nrm(ks[26], (DEPTH, 2, C_WIDTH), 0.02),
        'w_ri': nrm(ks[27], (DEPTH, 2, C_BLOCKS, C_BLOCK_W, C_BLOCK_W), C_BLOCK_W ** -0.5),
        'b_ri': nrm(ks[28], (DEPTH, 2, C_WIDTH), 0.02),
        'w_branch': nrm(ks[29], (DEPTH, N_BRANCH, MIX_W, D_MODEL), MIX_W ** -0.5),
        'w_out': nrm(ks[30], (DEPTH, D_MODEL, D_MODEL), D_MODEL ** -0.5),
        'w_ff1': nrm(ks[31], (DEPTH, D_MODEL, D_FF), D_MODEL ** -0.5),
        'w_ff2': nrm(ks[32], (DEPTH, D_FF, D_MODEL), D_FF ** -0.5),
    }


def reference(x_prompt, x_sample, cache_attn_k, cache_attn_v, cache_mla_ckv, cache_mla_krope, state_lru,
              c, c_ctx, w_ada, b_ada, g_pre1, g_post1, g_pre2, g_post2, w_in, sink, g_q, w_q_up,
              g_kv, w_uk, w_uv, w_conv, b_conv, lam, w_ra, b_ra, w_ri, b_ri, w_branch, w_out,
              w_ff1, w_ff2):
    n = x_sample.shape[1]
    n_rows = n // GRID_W
    rows = jnp.repeat(jnp.arange(n_rows), GRID_W)
    cols = jnp.arange(n_rows * GRID_W) % GRID_W
    y_p = x_prompt
    y_s = x_sample
    ks_l, vs_l, ckv_l, kr_l, st_l = [], [], [], [], []
    for l in range(DEPTH):
        lp = dict(w_ada=w_ada[l], b_ada=b_ada[l], g_pre1=g_pre1[l], g_post1=g_post1[l],
                  g_pre2=g_pre2[l], g_post2=g_post2[l], w_in=w_in[l], sink=sink[l], g_q=g_q[l],
                  w_q_up=w_q_up[l], g_kv=g_kv[l], w_uk=w_uk[l], w_uv=w_uv[l], w_conv=w_conv[l],
                  b_conv=b_conv[l], lam=lam[l], w_ra=w_ra[l], b_ra=b_ra[l], w_ri=w_ri[l], b_ri=b_ri[l],
                  w_branch=w_branch[l], w_out=w_out[l], w_ff1=w_ff1[l], w_ff2=w_ff2[l])
        y_p, k_new, v_new, ckv_new, kr_new, st_new = context_layer(y_p, c_ctx, lp)
        ks_l.append(k_new)
        vs_l.append(v_new)
        ckv_l.append(ckv_new)
        kr_l.append(kr_new)
        st_l.append(st_new)
        y_s = latent_layer(y_s, c, cache_attn_k[:, l], cache_attn_v[:, l], cache_mla_ckv[:, l],
                           cache_mla_krope[:, l], state_lru[:, l], rows, cols, lp)
    new_attn_k = jnp.stack(ks_l, axis=1)
    new_attn_v = jnp.stack(vs_l, axis=1)
    new_mla_ckv = jnp.stack(ckv_l, axis=1)
    new_mla_krope = jnp.stack(kr_l, axis=1)
    new_state_lru = jnp.stack(st_l, axis=1)
    return (y_p, y_s, new_attn_k, new_attn_v, new_mla_ckv, new_mla_krope, new_state_lru)
```

```python
import functools

import jax
import jax.numpy as jnp
from jax import lax
from jax.experimental import pallas as pl
from jax.experimental.pallas import tpu as pltpu

F32 = jnp.float32
BF16 = jnp.bfloat16

D_MODEL = 1024
DEPTH = 4
GRID_W = 64
Q_BLOCK = 128
A_HEADS = 8
A_KV_HEADS = 2
A_HEAD_DIM = 64
A_GROUP = A_HEADS // A_KV_HEADS
A_WINDOW = 128
B_HEADS = 8
B_Q_LORA = 384
B_KV_LORA = 256
B_NOPE = 64
B_ROPE = 32
B_QK_DIM = B_NOPE + B_ROPE
B_V_DIM = 64
C_WIDTH = 512
C_BLOCKS = 8
C_BLOCK_W = C_WIDTH // C_BLOCKS
C_CONV = 4
C_RG = 8.0
MIX_W = 512
N_BRANCH = 3
D_FF = 4 * D_MODEL
N_ADA = 6
ROPE_BASE = 10000.0
EPS = 1e-6
NEG_INF = -1e30

LANES = 128
SUBLANES = 8
HEAD_PAD = 128
VMEM_LIMIT = 56 * 1024 * 1024


def _cparams(*sem):
    return pltpu.CompilerParams(dimension_semantics=sem, vmem_limit_bytes=VMEM_LIMIT)


def _dot(a, b):
    return jnp.dot(a, b, preferred_element_type=F32)


def _dot_nt(a, b):
    return lax.dot_general(a, b, (((1,), (1,)), ((), ())), preferred_element_type=F32)


def _rms(x, g):
    return x * lax.rsqrt(jnp.mean(x * x, axis=-1, keepdims=True) + EPS) * g


def _rope(x, c, s, half):
    lane = lax.broadcasted_iota(jnp.int32, x.shape, 1)
    lo = (lane & (2 * half - 1)) < half
    partner = jnp.where(lo, pltpu.roll(x, LANES - half, 1), pltpu.roll(x, half, 1))
    return x * c + partner * s


def _ada_kernel(c_ref, w_ref, b_ref, o_ref):
    c = c_ref[...]
    a = (c * jax.nn.sigmoid(c)).astype(BF16)
    o_ref[0] = _dot(a, w_ref[0].astype(BF16)) + b_ref[0]


def _ada_call(cmat, w_ada, b_ada):
    nb = cmat.shape[0]
    n_out = N_ADA * D_MODEL
    tn = 1536
    return pl.pallas_call(
        _ada_kernel,
        grid=(DEPTH, n_out // tn),
        in_specs=[
            pl.BlockSpec((nb, D_MODEL), lambda l, j: (0, 0)),
            pl.BlockSpec((1, D_MODEL, tn), lambda l, j: (l, 0, j)),
            pl.BlockSpec((1, 1, tn), lambda l, j: (l, 0, j)),
        ],
        out_specs=pl.BlockSpec((1, nb, tn), lambda l, j: (l, 0, j)),
        out_shape=jax.ShapeDtypeStruct((DEPTH, nb, n_out), F32),
        compiler_params=_cparams("arbitrary", "arbitrary"),
        name="ada_mod",
    )(cmat, w_ada, b_ada.reshape(DEPTH, 1, n_out))


def _inproj_kernel(*refs, latent):
    (x_ref, sh_ref, sc_ref, gpre_ref, wa_ref, wb_ref, wl_ref, wg_ref,
     gq_ref, wq_ref, gkv_ref, wuk_ref, wuv_ref) = refs[:13]
    refs = refs[13:]
    if latent:
        caq_ref, saq_ref, ca_ref, sa_ref, cbq_ref, sbq_ref, cb_ref, sb_ref = refs[:8]
        refs = refs[8:]
        qa_ref, ka_ref, va_ref, qb_ref, kb_ref, vb_ref, xc_ref, gc_ref, gl_ref = refs
    else:
        (qa_ref, ka_ref, va_ref, qb_ref, kb_ref, vb_ref, xc_ref, gc_ref, gl_ref,
         kaf_ref, vaf_ref, ckvn_ref, krf_ref) = refs

    x = x_ref[...]
    h = _rms(x, gpre_ref[...]) * (1.0 + sc_ref[0]) + sh_ref[0]
    hb = h.astype(BF16)

    pa = _dot(hb, wa_ref[...])
    qa = pa[:, :A_HEADS * A_HEAD_DIM]
    ka = pa[:, A_HEADS * A_HEAD_DIM:A_HEADS * A_HEAD_DIM + LANES]
    va = pa[:, A_HEADS * A_HEAD_DIM + LANES:]
    a_scale = A_HEAD_DIM ** -0.5
    if latent:
        caq, saq = caq_ref[...], saq_ref[...]
        qa = jnp.concatenate(
            [_rope(qa[:, j * LANES:(j + 1) * LANES], caq, saq, 16) for j in range(4)], axis=1)
        ka = _rope(ka, ca_ref[...], sa_ref[...], 16)
    else:
        qa = qa * a_scale
        kaf_ref[...] = ka
        vaf_ref[...] = va
    qa_ref[...] = qa.astype(BF16)
    ka_ref[...] = ka.astype(BF16)
    va_ref[...] = va.astype(BF16)

    pb = _dot(hb, wb_ref[...])
    cq = pb[:, :B_Q_LORA]
    krp = pb[:, B_Q_LORA:B_Q_LORA + LANES]
    ckv = pb[:, B_Q_LORA + LANES:]
    b_scale = B_QK_DIM ** -0.5
    q = _dot(_rms(cq, gq_ref[...]).astype(BF16), wq_ref[...])
    ckvn = _rms(ckv, gkv_ref[...])
    ckvb = ckvn.astype(BF16)
    kn = _dot(ckvb, wuk_ref[...])
    vb_ref[...] = _dot(ckvb, wuv_ref[...]).astype(BF16)
    if latent:
        cbq, sbq = cbq_ref[...], sbq_ref[...]
        q = jnp.concatenate(
            [_rope(q[:, j * LANES:(j + 1) * LANES], cbq, sbq, 8) for j in range(B_HEADS)], axis=1)
        krp = _rope(krp, cb_ref[...], sb_ref[...], 8)
    else:
        q = q * b_scale
        ckvn_ref[...] = ckvn
        krf_ref[...] = krp[:, B_NOPE:B_NOPE + B_ROPE]
    qb_ref[...] = q.astype(BF16)
    kb_ref[...] = jnp.concatenate(
        [kn[:, j * LANES:(j + 1) * LANES] + krp for j in range(B_HEADS)], axis=1).astype(BF16)

    pc = _dot(hb, wl_ref[...])
    xc_ref[...] = pc[:, :C_WIDTH]
    gc_ref[...] = pc[:, C_WIDTH:]
    gl_ref[...] = _dot(hb, wg_ref[...])


def _inproj_call(x, mod3, mod_row, lw, tables, *, latent, seq_len, tm):
    t = x.shape[0]
    n_tiles = t // tm
    row = lambda i: (i, 0)
    const = lambda i: (0, 0)
    modspec = lambda col: pl.BlockSpec((1, 1, D_MODEL), lambda i: (mod_row(i), 0, col))
    wspec = lambda w: pl.BlockSpec(w.shape, const)
    weights = [lw["g_pre1"], lw["wa"], lw["wb"], lw["wl"], lw["wg"],
               lw["g_q"], lw["wq"], lw["g_kv"], lw["wuk"], lw["wuv"]]
    in_specs = [pl.BlockSpec((tm, D_MODEL), row), modspec(0), modspec(1)] + [wspec(w) for w in weights]
    args = [x, mod3, mod3] + weights
    if latent:
        tiles_per_seq = seq_len // tm
        tspec = pl.BlockSpec((tm, LANES), lambda i: (i % tiles_per_seq, 0))
        in_specs += [tspec] * 8
        args += list(tables)
    widths = [(512, BF16), (LANES, BF16), (LANES, BF16), (B_HEADS * HEAD_PAD, BF16),
              (B_HEADS * HEAD_PAD, BF16), (B_HEADS * B_V_DIM, BF16),
              (C_WIDTH, F32), (C_WIDTH, F32), (N_BRANCH * D_MODEL, F32)]
    if not latent:
        widths += [(LANES, F32), (LANES, F32), (B_KV_LORA, F32), (B_ROPE, F32)]
    out_shape = [jax.ShapeDtypeStruct((t, w), dt) for w, dt in widths]
    out_specs = [pl.BlockSpec((tm, w), row) for w, _ in widths]
    return pl.pallas_call(
        functools.partial(_inproj_kernel, latent=latent),
        grid=(n_tiles,),
        in_specs=in_specs,
        out_specs=out_specs,
        out_shape=out_shape,
        compiler_params=_cparams("arbitrary"),
        name="inproj_lat" if latent else "inproj_ctx",
    )(*args)


def _mla_cache_kernel(ckv_ref, kr_ref, wuk_ref, wuv_ref, place_ref, k_ref, v_ref):
    ckv = ckv_ref[0, 0].astype(BF16)
    kn = _dot(ckv, wuk_ref[0])
    krp = _dot(kr_ref[0, 0].astype(BF16), place_ref[...])
    k_ref[0, 0] = jnp.concatenate(
        [kn[:, j * LANES:(j + 1) * LANES] + krp for j in range(B_HEADS)], axis=1).astype(BF16)
    v_ref[0, 0] = _dot(ckv, wuv_ref[0]).astype(BF16)


def _mla_cache_call(cache_ckv, cache_krope, wuk, wuv, place):
    nb, _, p, _ = cache_ckv.shape
    kw, vw = B_HEADS * HEAD_PAD, B_HEADS * B_V_DIM
    return pl.pallas_call(
        _mla_cache_kernel,
        grid=(DEPTH, nb),
        in_specs=[
            pl.BlockSpec((1, 1, p, B_KV_LORA), lambda l, b: (b, l, 0, 0)),
            pl.BlockSpec((1, 1, p, B_ROPE), lambda l, b: (b, l, 0, 0)),
            pl.BlockSpec((1, B_KV_LORA, kw), lambda l, b: (l, 0, 0)),
            pl.BlockSpec((1, B_KV_LORA, vw), lambda l, b: (l, 0, 0)),
            pl.BlockSpec((B_ROPE, LANES), lambda l, b: (0, 0)),
        ],
        out_specs=[
            pl.BlockSpec((1, 1, p, kw), lambda l, b: (b, l, 0, 0)),
            pl.BlockSpec((1, 1, p, vw), lambda l, b: (b, l, 0, 0)),
        ],
        out_shape=[jax.ShapeDtypeStruct((nb, DEPTH, p, kw), BF16),
                   jax.ShapeDtypeStruct((nb, DEPTH, p, vw), BF16)],
        compiler_params=_cparams("arbitrary", "arbitrary"),
        name="mla_cache_kv",
    )(cache_ckv, cache_krope, wuk, wuv, place)


def _gqa_kernel(*refs, latent, seq_len):
    if latent:
        sink_ref, q_ref, k_ref, v_ref, kc_ref, vc_ref, o_ref = refs
    else:
        sink_ref, q_ref, k_ref, v_ref, o_ref = refs
    q = q_ref[0]
    nq = q.shape[0]
    if latent:
        win = 3 * Q_BLOCK
        i = pl.program_id(1)
        start = pl.multiple_of(jnp.clip((i - 1) * Q_BLOCK, 0, seq_len - win), Q_BLOCK)
        kl = k_ref[0, pl.ds(start, win), :]
        vl = v_ref[0, pl.ds(start, win), :]
        qpos = i * Q_BLOCK + lax.broadcasted_iota(jnp.int32, (nq, win), 0)
        kpos = start + lax.broadcasted_iota(jnp.int32, (nq, win), 1)
        valid = jnp.abs(qpos - kpos) <= A_WINDOW
        kc = kc_ref[0, 0]
        vc = vc_ref[0, 0]
    else:
        kl = k_ref[0]
        vl = v_ref[0]
    outs = []
    for hd in range(A_HEADS):
        kh = hd // A_GROUP
        ks = slice(kh * A_HEAD_DIM, (kh + 1) * A_HEAD_DIM)
        qh = q[:, hd * A_HEAD_DIM:(hd + 1) * A_HEAD_DIM]
        sk = sink_ref[hd]
        s_l = _dot_nt(qh, kl[:, ks])
        if latent:
            s_l = jnp.where(valid, s_l, NEG_INF)
            s_c = _dot_nt(qh, kc[:, ks])
            m = jnp.maximum(jnp.max(s_l, axis=-1, keepdims=True), jnp.max(s_c, axis=-1, keepdims=True))
        else:
            m = jnp.max(s_l, axis=-1, keepdims=True)
        m = jnp.maximum(m, sk)
        p_l = jnp.exp(s_l - m)
        den = jnp.sum(p_l, axis=-1, keepdims=True) + jnp.exp(sk - m)
        o = _dot(p_l.astype(BF16), vl[:, ks])
        if latent:
            p_c = jnp.exp(s_c - m)
            den = den + jnp.sum(p_c, axis=-1, keepdims=True)
            o = o + _dot(p_c.astype(BF16), vc[:, ks])
        outs.append(o / den)
    o_ref[0] = jnp.concatenate(outs, axis=1).astype(BF16)


def _gqa_call(sink_l, q, k, v, kc=None, vc=None, layer=0):
    nb, n, _ = q.shape
    latent = kc is not None
    qw = A_HEADS * A_HEAD_DIM
    smem = pl.BlockSpec(memory_space=pltpu.SMEM)
    if latent:
        p = kc.shape[2]
        grid = (nb, n // Q_BLOCK)
        in_specs = [smem,
                    pl.BlockSpec((1, Q_BLOCK, qw), lambda b, i: (b, i, 0)),
                    pl.BlockSpec((1, n, LANES), lambda b, i: (b, 0, 0)),
                    pl.BlockSpec((1, n, LANES), lambda b, i: (b, 0, 0)),
                    pl.BlockSpec((1, 1, p, LANES), lambda b, i: (b, layer, 0, 0)),
                    pl.BlockSpec((1, 1, p, LANES), lambda b, i: (b, layer, 0, 0))]
        out_spec = pl.BlockSpec((1, Q_BLOCK, qw), lambda b, i: (b, i, 0))
        args = (sink_l, q, k, v, kc, vc)
    else:
        grid = (nb, 1)
        in_specs = [smem,
                    pl.BlockSpec((1, n, qw), lambda b, i: (b, 0, 0)),
                    pl.BlockSpec((1, n, LANES), lambda b, i: (b, 0, 0)),
                    pl.BlockSpec((1, n, LANES), lambda b, i: (b, 0, 0))]
        out_spec = pl.BlockSpec((1, n, qw), lambda b, i: (b, 0, 0))
        args = (sink_l, q, k, v)
    return pl.pallas_call(
        functools.partial(_gqa_kernel, latent=latent, seq_len=n),
        grid=grid,
        in_specs=in_specs,
        out_specs=out_spec,
        out_shape=jax.ShapeDtypeStruct((nb, n, qw), BF16),
        compiler_params=_cparams("arbitrary", "arbitrary"),
        name="gqa_lat" if latent else "gqa_ctx",
    )(*args)


def _mla_kernel(*refs, latent, n_chunks, tk):
    if latent:
        q_ref, k_ref, v_ref, kc_ref, vc_ref, o_ref = refs
    else:
        q_ref, k_ref, v_ref, o_ref = refs
    tq = q_ref.shape[1]
    outs = []
    for hd in range(B_HEADS):
        kcol = slice(hd * HEAD_PAD, (hd + 1) * HEAD_PAD)
        vcol = slice(hd * B_V_DIM, (hd + 1) * B_V_DIM)
        qh = q_ref[0, :, kcol]

        def update(carry, kt, vt):
            m, l, acc = carry
            s = _dot_nt(qh, kt)
            m_new = jnp.maximum(m, jnp.max(s, axis=-1, keepdims=True))
            alpha = jnp.exp(m - m_new)
            p = jnp.exp(s - m_new)
            l = alpha * l + jnp.sum(p, axis=-1, keepdims=True)
            acc = alpha * acc + _dot(p.astype(BF16), vt)
            return m_new, l, acc

        def body(c, carry):
            off = pl.multiple_of(c * tk, tk)
            return update(carry, k_ref[0, pl.ds(off, tk), kcol], v_ref[0, pl.ds(off, tk), vcol])

        carry = (jnp.full((tq, 1), -jnp.inf, F32), jnp.zeros((tq, 1), F32),
                 jnp.zeros((tq, B_V_DIM), F32))
        if n_chunks == 1:
            carry = update(carry, k_ref[0, :, kcol], v_ref[0, :, vcol])
        else:
            carry = lax.fori_loop(0, n_chunks, body, carry)
        if latent:
            carry = update(carry, kc_ref[0, 0, :, kcol], vc_ref[0, 0, :, vcol])
        _, l, acc = carry
        outs.append(acc / l)
    o_ref[0] = jnp.concatenate(outs, axis=1).astype(BF16)


def _mla_call(q, k, v, kc=None, vc=None, layer=0, *, tq, tk):
    nb, n, kw = k.shape
    vw = v.shape[2]
    latent = kc is not None
    in_specs = [pl.BlockSpec((1, tq, kw), lambda b, i: (b, i, 0)),
                pl.BlockSpec((1, n, kw), lambda b, i: (b, 0, 0)),
                pl.BlockSpec((1, n, vw), lambda b, i: (b, 0, 0))]
    args = [q, k, v]
    if latent:
        p = kc.shape[2]
        in_specs += [pl.BlockSpec((1, 1, p, kw), lambda b, i: (b, layer, 0, 0)),
                     pl.BlockSpec((1, 1, p, vw), lambda b, i: (b, layer, 0, 0))]
        args += [kc, vc]
    return pl.pallas_call(
        functools.partial(_mla_kernel, latent=latent, n_chunks=n // tk, tk=tk),
        grid=(nb, n // tq),
        in_specs=in_specs,
        out_specs=pl.BlockSpec((1, tq, vw), lambda b, i: (b, i, 0)),
        out_shape=jax.ShapeDtypeStruct((nb, n, vw), BF16),
        compiler_params=_cparams("arbitrary", "arbitrary"),
        name="mla_lat" if latent else "mla_ctx",
    )(*args)


def _lru_kernel(xc_ref, gc_ref, wconv_ref, bconv_ref, lam_ref, wg_ref, bg_ref, h0_ref,
                y_ref, st_ref, xpad, af, uf, ab, ub, *, seq_len, rows):
    n = seq_len
    lc = n // SUBLANES
    pad = SUBLANES
    xpad[0:pad, :] = jnp.zeros((pad, LANES), F32)
    xpad[pad + n:pad + n + pad, :] = jnp.zeros((pad, LANES), F32)
    xpad[pad:pad + n, :] = xc_ref[0]

    wconv = wconv_ref[...]
    bconv = bconv_ref[...]
    lam = lam_ref[...]
    z = -lam
    softplus = jnp.maximum(z, 0.0) + jnp.log1p(jnp.exp(-jnp.abs(z)))
    neg_c_sp = -C_RG * softplus
    wg = wg_ref[0]
    bg = bg_ref[0]

    for r0 in range(0, n, rows):
        xconv = bconv
        for j in range(C_CONV):
            xconv = xconv + xpad[pad + r0 + j - 1:pad + r0 + j - 1 + rows, :] * wconv[j:j + 1, :]
        g = _dot(xconv.astype(BF16), wg) + bg
        for d, (a_ref, u_ref) in enumerate(((af, uf), (ab, ub))):
            r = jax.nn.sigmoid(g[:, (2 * d) * LANES:(2 * d + 1) * LANES])
            ig = jax.nn.sigmoid(g[:, (2 * d + 1) * LANES:(2 * d + 2) * LANES])
            log_a = neg_c_sp[d:d + 1, :] * r
            a = jnp.exp(log_a)
            mult = jnp.sqrt(jnp.tanh(-log_a) * (1.0 + a * a))
            a_ref[r0:r0 + rows, :] = a
            u_ref[r0:r0 + rows, :] = mult * (ig * xconv)

    def fwd_body(t, carry):
        hloc, acum = carry
        idx = pl.ds(t, SUBLANES, stride=lc)
        a_t = af[idx, :]
        hloc = a_t * hloc + uf[idx, :]
        acum = a_t * acum
        uf[idx, :] = hloc
        af[idx, :] = acum
        return hloc, acum

    def bwd_body(j, carry):
        hloc, acum = carry
        idx = pl.ds(lc - 1 - j, SUBLANES, stride=lc)
        a_t = ab[idx, :]
        hloc = a_t * hloc + ub[idx, :]
        acum = a_t * acum
        ub[idx, :] = hloc
        ab[idx, :] = acum
        return hloc, acum

    init = (jnp.zeros((SUBLANES, LANES), F32), jnp.ones((SUBLANES, LANES), F32))
    hf_end, af_end = lax.fori_loop(0, lc, fwd_body, init)
    hb_end, ab_end = lax.fori_loop(0, lc, bwd_body, init)

    h0 = h0_ref[0]
    cf = [h0[0:1, :]]
    for s in range(SUBLANES):
        cf.append(af_end[s:s + 1, :] * cf[s] + hf_end[s:s + 1, :])
    cb = [None] * (SUBLANES + 1)
    cb[SUBLANES] = h0[1:2, :]
    for s in range(SUBLANES - 1, -1, -1):
        cb[s] = ab_end[s:s + 1, :] * cb[s + 1] + hb_end[s:s + 1, :]
    st_ref[0] = jnp.concatenate([cf[SUBLANES], cb[0]], axis=0)

    for s in range(SUBLANES):
        rs = slice(s * lc, (s + 1) * lc)
        hf = uf[rs, :] + af[rs, :] * cf[s]
        hb = ub[rs, :] + ab[rs, :] * cb[s + 1]
        y_ref[0, rs, :] = (jax.nn.gelu(gc_ref[0, rs, :]) * (hf + hb)).astype(BF16)


def _lru_call(xc, gc, lw, h0):
    nb, n, _ = xc.shape
    ng = C_WIDTH // LANES
    rows = min(n, 512)
    seq = lambda b, g: (b, 0, g)
    return pl.pallas_call(
        functools.partial(_lru_kernel, seq_len=n, rows=rows),
        grid=(nb, ng),
        in_specs=[
            pl.BlockSpec((1, n, LANES), seq),
            pl.BlockSpec((1, n, LANES), seq),
            pl.BlockSpec((C_CONV, LANES), lambda b, g: (0, g)),
            pl.BlockSpec((1, LANES), lambda b, g: (0, g)),
            pl.BlockSpec((2, LANES), lambda b, g: (0, g)),
            pl.BlockSpec((1, LANES, 4 * LANES), lambda b, g: (g, 0, 0)),
            pl.BlockSpec((1, 1, 4 * LANES), lambda b, g: (g, 0, 0)),
            pl.BlockSpec((1, 2, LANES), seq),
        ],
        out_specs=[pl.BlockSpec((1, n, LANES), seq), pl.BlockSpec((1, 2, LANES), seq)],
        out_shape=[jax.ShapeDtypeStruct((nb, n, C_WIDTH), BF16),
                   jax.ShapeDtypeStruct((nb, 2, C_WIDTH), F32)],
        scratch_shapes=[pltpu.VMEM((n + 2 * SUBLANES, LANES), F32)] + [pltpu.VMEM((n, LANES), F32)] * 4,
        compiler_params=_cparams("arbitrary", "arbitrary"),
        name="rglru",
    )(xc, gc, lw["w_conv"], lw["b_conv"], lw["lam"], lw["wgate"], lw["bgate"], h0)


def _merge_kernel(x_ref, ya_ref, yb_ref, yc_ref, gl_ref, g1_ref, gpost_ref, wbr_ref, wo_ref, o_ref):
    m = None
    for nbr, y_ref in enumerate((ya_ref, yb_ref, yc_ref)):
        gate = jax.nn.sigmoid(gl_ref[:, nbr * D_MODEL:(nbr + 1) * D_MODEL])
        term = gate * _dot(y_ref[...], wbr_ref[nbr])
        m = term if m is None else m + term
    o = _dot(m.astype(BF16), wo_ref[...])
    o_ref[...] = x_ref[...] + g1_ref[0] * _rms(o, gpost_ref[...])


def _merge_call(x, ya, yb, yc, gl, mod3, mod_row, lw, *, tm):
    t = x.shape[0]
    row = lambda i: (i, 0)
    return pl.pallas_call(
        _merge_kernel,
        grid=(t // tm,),
        in_specs=[
            pl.BlockSpec((tm, D_MODEL), row),
            pl.BlockSpec((tm, MIX_W), row),
            pl.BlockSpec((tm, MIX_W), row),
            pl.BlockSpec((tm, MIX_W), row),
            pl.BlockSpec((tm, N_BRANCH * D_MODEL), row),
            pl.BlockSpec((1, 1, D_MODEL), lambda i: (mod_row(i), 0, 2)),
            pl.BlockSpec((1, D_MODEL), lambda i: (0, 0)),
            pl.BlockSpec((N_BRANCH, MIX_W, D_MODEL), lambda i: (0, 0, 0)),
            pl.BlockSpec((D_MODEL, D_MODEL), lambda i: (0, 0)),
        ],
        out_specs=pl.BlockSpec((tm, D_MODEL), row),
        out_shape=jax.ShapeDtypeStruct((t, D_MODEL), F32),
        compiler_params=_cparams("arbitrary"),
        name="merge",
    )(x, ya, yb, yc, gl, mod3, lw["g_post1"], lw["w_branch"], lw["w_out"])


def _mlp_kernel(x_ref, sh_ref, sc_ref, g2_ref, gpre_ref, gpost_ref, w1_ref, w2_ref, o_ref, *, ff_chunk):
    x = x_ref[...]
    hb = (_rms(x, gpre_ref[...]) * (1.0 + sc_ref[0]) + sh_ref[0]).astype(BF16)
    f = None
    for c0 in range(0, D_FF, ff_chunk):
        u = jnp.maximum(_dot(hb, w1_ref[:, c0:c0 + ff_chunk]), 0.0)
        part = _dot((u * u).astype(BF16), w2_ref[c0:c0 + ff_chunk, :])
        f = part if f is None else f + part
    o_ref[...] = x + g2_ref[0] * _rms(f, gpost_ref[...])


def _mlp_call(x, mod3, mod_row, lw, *, tm):
    t = x.shape[0]
    row = lambda i: (i, 0)
    modspec = lambda col: pl.BlockSpec((1, 1, D_MODEL), lambda i: (mod_row(i), 0, col))
    return pl.pallas_call(
        functools.partial(_mlp_kernel, ff_chunk=1024),
        grid=(t // tm,),
        in_specs=[
            pl.BlockSpec((tm, D_MODEL), row),
            modspec(3), modspec(4), modspec(5),
            pl.BlockSpec((1, D_MODEL), lambda i: (0, 0)),
            pl.BlockSpec((1, D_MODEL), lambda i: (0, 0)),
            pl.BlockSpec((D_MODEL, D_FF), lambda i: (0, 0)),
            pl.BlockSpec((D_FF, D_MODEL), lambda i: (0, 0)),
        ],
        out_specs=pl.BlockSpec((tm, D_MODEL), row),
        out_shape=jax.ShapeDtypeStruct((t, D_MODEL), F32),
        compiler_params=_cparams("arbitrary"),
        name="mlp",
    )(x, mod3, mod3, mod3, lw["g_pre2"], lw["g_post2"], lw["w_ff1"], lw["w_ff2"])


def _rope_tables(n):
    pos = jnp.arange(n)
    rows = (pos // GRID_W).astype(F32)[:, None]
    cols = (pos % GRID_W).astype(F32)[:, None]

    def pattern(dim):
        half = dim // 4
        inv = jnp.power(ROPE_BASE, -jnp.arange(half, dtype=F32) * (2.0 / (dim // 2)))
        ar, ac = rows * inv, cols * inv
        c = jnp.concatenate([jnp.cos(ar), jnp.cos(ar), jnp.cos(ac), jnp.cos(ac)], axis=1)
        s = jnp.concatenate([-jnp.sin(ar), jnp.sin(ar), -jnp.sin(ac), jnp.sin(ac)], axis=1)
        return c, s

    c64, s64 = pattern(A_HEAD_DIM)
    ca = jnp.tile(c64, (1, LANES // A_HEAD_DIM))
    sa = jnp.tile(s64, (1, LANES // A_HEAD_DIM))
    c32, s32 = pattern(B_ROPE)
    ones = jnp.ones((n, B_NOPE), F32)
    zeros = jnp.zeros((n, B_NOPE), F32)
    tail = HEAD_PAD - B_QK_DIM
    cb = jnp.concatenate([ones, c32, jnp.ones((n, tail), F32)], axis=1)
    sb = jnp.concatenate([zeros, s32, jnp.zeros((n, tail), F32)], axis=1)
    a_scale = A_HEAD_DIM ** -0.5
    b_scale = B_QK_DIM ** -0.5
    return (ca * a_scale, sa * a_scale, ca, sa, cb * b_scale, sb * b_scale, cb, sb)


def _prep_weights(p):
    w_in = p["w_in"]
    o_cq = 768
    o_ckv = o_cq + B_Q_LORA
    o_kr = o_ckv + B_KV_LORA
    o_xc = o_kr + B_ROPE
    o_gl = o_xc + 2 * C_WIDTH
    kr_placed = jnp.pad(w_in[:, :, o_kr:o_xc], ((0, 0), (0, 0), (B_NOPE, HEAD_PAD - B_QK_DIM)))
    wb = jnp.concatenate([w_in[:, :, o_cq:o_ckv], kr_placed, w_in[:, :, o_ckv:o_kr]], axis=2)
    pad_heads = lambda w, dh: jnp.pad(
        w.reshape(DEPTH, w.shape[1], B_HEADS, dh), ((0, 0), (0, 0), (0, 0), (0, HEAD_PAD - dh))
    ).reshape(DEPTH, w.shape[1], B_HEADS * HEAD_PAD)

    eye = jnp.eye(C_BLOCKS, dtype=F32)

    def gate_groups(w):
        full = (w[:, :, :, None, :] * eye[None, :, None, :, None]).reshape(DEPTH, C_WIDTH, C_WIDTH)
        ng = C_WIDTH // LANES
        full = full.reshape(DEPTH, ng, LANES, ng, LANES)
        return jnp.stack([full[:, g, :, g, :] for g in range(ng)], axis=1)

    w_ra, w_ri, b_ra, b_ri = p["w_ra"], p["w_ri"], p["b_ra"], p["b_ri"]
    wgate = jnp.concatenate([gate_groups(w_ra[:, 0]), gate_groups(w_ri[:, 0]),
                             gate_groups(w_ra[:, 1]), gate_groups(w_ri[:, 1])], axis=3)
    ng = C_WIDTH // LANES
    grp = lambda b: b.reshape(DEPTH, ng, 1, LANES)
    bgate = jnp.concatenate([grp(b_ra[:, 0]), grp(b_ri[:, 0]), grp(b_ra[:, 1]), grp(b_ri[:, 1])], axis=3)

    row = lambda g: g[:, None, :]
    return dict(
        g_pre1=row(p["g_pre1"]), g_post1=row(p["g_post1"]), g_pre2=row(p["g_pre2"]), g_post2=row(p["g_post2"]),
        g_q=row(p["g_q"]), g_kv=row(p["g_kv"]),
        wa=w_in[:, :, :o_cq].astype(BF16), wb=wb.astype(BF16),
        wl=w_in[:, :, o_xc:o_gl].astype(BF16), wg=w_in[:, :, o_gl:].astype(BF16),
        wq=pad_heads(p["w_q_up"], B_QK_DIM).astype(BF16),
        wuk=pad_heads(p["w_uk"], B_NOPE).astype(BF16),
        wuv=p["w_uv"].astype(BF16),
        w_conv=p["w_conv"], b_conv=row(p["b_conv"]), lam=p["lam"],
        wgate=wgate.astype(BF16), bgate=bgate,
        w_branch=p["w_branch"].astype(BF16), w_out=p["w_out"].astype(BF16),
        w_ff1=p["w_ff1"].astype(BF16), w_ff2=p["w_ff2"].astype(BF16),
    )


def _pick_tile(n, target):
    t = min(n, target)
    while n % t:
        t //= 2
    return t


def kernel(x_prompt, x_sample, cache_attn_k, cache_attn_v, cache_mla_ckv, cache_mla_krope, state_lru, c, c_ctx, w_ada, b_ada, g_pre1, g_post1, g_pre2, g_post2, w_in, sink, g_q, w_q_up, g_kv, w_uk, w_uv, w_conv, b_conv, lam, w_ra, b_ra, w_ri, b_ri, w_branch, w_out, w_ff1, w_ff2):
    nbc, seq, _ = x_prompt.shape
    nbl, n_lat, _ = x_sample.shape
    past = cache_attn_k.shape[2]
    assert n_lat % GRID_W == 0 and n_lat >= 3 * Q_BLOCK and n_lat % Q_BLOCK == 0
    assert seq % (SUBLANES * SUBLANES) == 0 and n_lat % (SUBLANES * SUBLANES) == 0

    weights = _prep_weights(dict(
        g_pre1=g_pre1, g_post1=g_post1, g_pre2=g_pre2, g_post2=g_post2, w_in=w_in, g_q=g_q,
        w_q_up=w_q_up, g_kv=g_kv, w_uk=w_uk, w_uv=w_uv, w_conv=w_conv, b_conv=b_conv, lam=lam,
        w_ra=w_ra, b_ra=b_ra, w_ri=w_ri, b_ri=b_ri, w_branch=w_branch, w_out=w_out,
        w_ff1=w_ff1, w_ff2=w_ff2))
    tables = _rope_tables(n_lat)

    n_mod = -(-(nbl + 1) // SUBLANES) * SUBLANES
    cmat = jnp.zeros((n_mod, D_MODEL), F32).at[:nbl].set(c).at[nbl].set(c_ctx)
    mod3 = _ada_call(cmat, w_ada, b_ada).reshape(DEPTH * n_mod, 1, N_ADA * D_MODEL)

    place = jnp.pad(jnp.eye(B_ROPE, dtype=BF16), ((0, 0), (B_NOPE, HEAD_PAD - B_QK_DIM)))
    kc_mla, vc_mla = _mla_cache_call(cache_mla_ckv, cache_mla_krope, weights["wuk"], weights["wuv"], place)
    kc_a = cache_attn_k.reshape(nbl, DEPTH, past, LANES).astype(BF16)
    vc_a = cache_attn_v.reshape(nbl, DEPTH, past, LANES).astype(BF16)

    tm_p = _pick_tile(nbc * seq, 256)
    tm_s = _pick_tile(n_lat, 256)
    y_p = x_prompt.reshape(nbc * seq, D_MODEL)
    y_s = x_sample.reshape(nbl * n_lat, D_MODEL)
    zero_state = jnp.zeros((nbc, 2, C_WIDTH), F32)
    ks_l, vs_l, ckv_l, kr_l, st_l = [], [], [], [], []
    for l in range(DEPTH):
        lw = {k: v[l] for k, v in weights.items()}
        row_p = lambda i, l=l: l * n_mod + nbl
        row_s = lambda i, l=l: l * n_mod + (i * tm_s) // n_lat

        (qa, ka, va, qb, kb, vb, xc, gc, gl, kaf, vaf, ckvn, krf) = _inproj_call(
            y_p, mod3, row_p, lw, None, latent=False, seq_len=seq, tm=tm_p)
        r3 = lambda a: a.reshape(nbc, seq, a.shape[-1])
        ya = _gqa_call(sink[l], r3(qa), r3(ka), r3(va))
        yb = _mla_call(r3(qb), r3(kb), r3(vb), tq=seq, tk=seq)
        yc, st = _lru_call(r3(xc), r3(gc), lw, zero_state)
        f2 = lambda a: a.reshape(nbc * seq, a.shape[-1])
        y_p = _merge_call(y_p, f2(ya), f2(yb), f2(yc), gl, mod3, row_p, lw, tm=tm_p)
        y_p = _mlp_call(y_p, mod3, row_p, lw, tm=tm_p)
        ks_l.append(kaf.reshape(nbc, seq, A_KV_HEADS, A_HEAD_DIM))
        vs_l.append(vaf.reshape(nbc, seq, A_KV_HEADS, A_HEAD_DIM))
        ckv_l.append(ckvn.reshape(nbc, seq, B_KV_LORA))
        kr_l.append(krf.reshape(nbc, seq, B_ROPE))
        st_l.append(st)

        (qa, ka, va, qb, kb, vb, xc, gc, gl) = _inproj_call(
            y_s, mod3, row_s, lw, tables, latent=True, seq_len=n_lat, tm=tm_s)
        r3 = lambda a: a.reshape(nbl, n_lat, a.shape[-1])
        ya = _gqa_call(sink[l], r3(qa), r3(ka), r3(va), kc_a, vc_a, layer=l)
        yb = _mla_call(r3(qb), r3(kb), r3(vb), kc_mla, vc_mla, layer=l,
                       tq=_pick_tile(n_lat, 256), tk=_pick_tile(n_lat, 512))
        yc, _ = _lru_call(r3(xc), r3(gc), lw, state_lru[:, l])
        f2 = lambda a: a.reshape(nbl * n_lat, a.shape[-1])
        y_s = _merge_call(y_s, f2(ya), f2(yb), f2(yc), gl, mod3, row_s, lw, tm=tm_s)
        y_s = _mlp_call(y_s, mod3, row_s, lw, tm=tm_s)

    return (y_p.reshape(nbc, seq, D_MODEL), y_s.reshape(nbl, n_lat, D_MODEL),
            jnp.stack(ks_l, axis=1), jnp.stack(vs_l, axis=1), jnp.stack(ckv_l, axis=1),
            jnp.stack(kr_l, axis=1), jnp.stack(st_l, axis=1))
```

```python
import functools
import math

import jax
import jax.numpy as jnp
from jax import lax
from jax.experimental import pallas as pl
from jax.experimental.pallas import tpu as pltpu

F32 = jnp.float32
BF16 = jnp.bfloat16

D_MODEL = 1024
DEPTH = 4
GRID_W = 64
Q_BLOCK = 128
A_HEADS = 8
A_KV_HEADS = 2
A_HEAD_DIM = 64
A_GROUP = A_HEADS // A_KV_HEADS
A_WINDOW = 128
B_HEADS = 8
B_Q_LORA = 384
B_KV_LORA = 256
B_NOPE = 64
B_ROPE = 32
B_QK_DIM = B_NOPE + B_ROPE
B_V_DIM = 64
C_WIDTH = 512
C_BLOCKS = 8
C_BLOCK_W = C_WIDTH // C_BLOCKS
C_CONV = 4
C_RG = 8.0
MIX_W = 512
N_BRANCH = 3
D_FF = 4 * D_MODEL
N_ADA = 6
ROPE_BASE = 10000.0
EPS = 1e-6
NEG_INF = -1e30
LOG2E = math.log2(math.e)

LANES = 128
SUBLANES = 8
HEAD_PAD = 128
VMEM_LIMIT = 56 * 1024 * 1024
A_QSCALE = A_HEAD_DIM ** -0.5 * LOG2E
B_QSCALE = B_QK_DIM ** -0.5 * LOG2E


def _cparams(*sem):
    return pltpu.CompilerParams(dimension_semantics=sem, vmem_limit_bytes=VMEM_LIMIT)


def _dot(a, b):
    return jnp.dot(a, b, preferred_element_type=F32)


def _dot_nt(a, b):
    return lax.dot_general(a, b, (((1,), (1,)), ((), ())), preferred_element_type=F32)


def _rms(x, g):
    return x * lax.rsqrt(jnp.mean(x * x, axis=-1, keepdims=True) + EPS) * g


def _rope(x, c, s, half, axis):
    pos = lax.broadcasted_iota(jnp.int32, x.shape, axis)
    lo = (pos & (2 * half - 1)) < half
    partner = jnp.where(lo, pltpu.roll(x, LANES - half, axis), pltpu.roll(x, half, axis))
    return x * c + partner * s


def _ada_kernel(c_ref, w_ref, b_ref, o_ref):
    c = c_ref[...]
    a = (c * jax.nn.sigmoid(c)).astype(BF16)
    o_ref[0] = _dot(a, w_ref[0].astype(BF16)) + b_ref[0]


def _ada_call(cmat, w_ada, b_ada):
    nb = cmat.shape[0]
    n_out = N_ADA * D_MODEL
    tn = 1536
    return pl.pallas_call(
        _ada_kernel,
        grid=(DEPTH, n_out // tn),
        in_specs=[
            pl.BlockSpec((nb, D_MODEL), lambda l, j: (0, 0)),
            pl.BlockSpec((1, D_MODEL, tn), lambda l, j: (l, 0, j)),
            pl.BlockSpec((1, 1, tn), lambda l, j: (l, 0, j)),
        ],
        out_specs=pl.BlockSpec((1, nb, tn), lambda l, j: (l, 0, j)),
        out_shape=jax.ShapeDtypeStruct((DEPTH, nb, n_out), F32),
        compiler_params=_cparams("arbitrary", "arbitrary"),
        name="ada_mod",
    )(cmat, w_ada, b_ada.reshape(DEPTH, 1, n_out))


def _inproj_kernel(*refs, latent):
    (x_ref, sh_ref, sc_ref, gpre_ref, wa_ref, wkat_ref, wb_ref, wkrt_ref, wl_ref, wg_ref,
     gq_ref, wq_ref, gkv_ref, wukt_ref, wuv_ref) = refs[:15]
    refs = refs[15:]
    if latent:
        caq_ref, saq_ref, cbq_ref, sbq_ref, cat_ref, sat_ref, cbt_ref, sbt_ref = refs[:8]
        refs = refs[8:]
        qa_ref, kat_ref, va_ref, qb_ref, kbt_ref, vb_ref, xc_ref, gc_ref, gl_ref = refs
    else:
        (qa_ref, kat_ref, va_ref, qb_ref, kbt_ref, vb_ref, xc_ref, gc_ref, gl_ref,
         kaf_ref, vaf_ref, ckvn_ref, krf_ref) = refs

    x = x_ref[...]
    h = _rms(x, gpre_ref[...]) * (1.0 + sc_ref[0]) + sh_ref[0]
    hb = h.astype(BF16)

    qw = A_HEADS * A_HEAD_DIM
    pa = _dot(hb, wa_ref[...])
    qa = pa[:, :qw]
    kat = _dot_nt(wkat_ref[...], hb)
    if latent:
        va = pa[:, qw:]
        caq, saq = caq_ref[...], saq_ref[...]
        qa = jnp.concatenate(
            [_rope(qa[:, j * LANES:(j + 1) * LANES], caq, saq, 16, 1) for j in range(qw // LANES)], axis=1)
        kat = _rope(kat, cat_ref[...], sat_ref[...], 16, 0)
    else:
        va = pa[:, qw + LANES:]
        kaf_ref[...] = pa[:, qw:qw + LANES]
        vaf_ref[...] = va
        qa = qa * A_QSCALE
    for hd in range(A_HEADS):
        qa_ref[0, hd] = qa[:, hd * A_HEAD_DIM:(hd + 1) * A_HEAD_DIM].astype(BF16)
    kat_ref[0] = kat.astype(BF16)
    va_ref[...] = va.astype(BF16)

    pb = _dot(hb, wb_ref[...])
    cq = pb[:, :B_Q_LORA]
    ckv = pb[:, pb.shape[1] - B_KV_LORA:]
    q = _dot(_rms(cq, gq_ref[...]).astype(BF16), wq_ref[...])
    ckvn = _rms(ckv, gkv_ref[...])
    ckvb = ckvn.astype(BF16)
    knt = _dot_nt(wukt_ref[...], ckvb)
    krt = _dot_nt(wkrt_ref[...], hb)
    vv = _dot(ckvb, wuv_ref[...])
    for hd in range(B_HEADS):
        vb_ref[0, hd] = vv[:, hd * B_V_DIM:(hd + 1) * B_V_DIM].astype(BF16)
    if latent:
        cbq, sbq = cbq_ref[...], sbq_ref[...]
        q = jnp.concatenate(
            [_rope(q[:, j * LANES:(j + 1) * LANES], cbq, sbq, 8, 1) for j in range(B_HEADS)], axis=1)
        krt = _rope(krt, cbt_ref[...], sbt_ref[...], 8, 0)
    else:
        q = q * B_QSCALE
        ckvn_ref[...] = ckvn
        krf_ref[...] = pb[:, B_Q_LORA + B_NOPE:B_Q_LORA + B_QK_DIM]
    qb_ref[...] = q.astype(BF16)
    kbt_ref[0] = jnp.concatenate(
        [knt[j * HEAD_PAD:(j + 1) * HEAD_PAD, :] + krt for j in range(B_HEADS)], axis=0).astype(BF16)

    pc = _dot(hb, wl_ref[...])
    xc_ref[...] = pc[:, :C_WIDTH]
    gc_ref[...] = jax.nn.gelu(pc[:, C_WIDTH:])
    gl_ref[...] = _dot(hb, wg_ref[...])


def _inproj_call(x, mod3, mod_row, lw, tables, *, latent, seq_len, tm):
    t = x.shape[0]
    nb = t // seq_len
    tps = seq_len // tm
    row = lambda i: (i, 0)
    const = lambda i: (0, 0)
    modspec = lambda col: pl.BlockSpec((1, 1, D_MODEL), lambda i: (mod_row(i), 0, col))
    wspec = lambda w: pl.BlockSpec(w.shape, const)
    sfx = "_lat" if latent else "_ctx"
    weights = [lw["g_pre1"], lw["wa" + sfx], lw["wkat"], lw["wb" + sfx], lw["wkrt"], lw["wl"], lw["wg"],
               lw["g_q"], lw["wq"], lw["g_kv"], lw["wukt"], lw["wuv"]]
    in_specs = [pl.BlockSpec((tm, D_MODEL), row), modspec(0), modspec(1)] + [wspec(w) for w in weights]
    args = [x, mod3, mod3] + weights
    if latent:
        rspec = pl.BlockSpec((tm, LANES), lambda i: (i % tps, 0))
        cspec = pl.BlockSpec((LANES, tm), lambda i: (0, i % tps))
        in_specs += [rspec] * 4 + [cspec] * 4
        args += list(tables)
    rows2 = lambda w, dt: (jax.ShapeDtypeStruct((t, w), dt), pl.BlockSpec((tm, w), row))
    kw = B_HEADS * HEAD_PAD
    outs = [
        (jax.ShapeDtypeStruct((nb, A_HEADS, seq_len, A_HEAD_DIM), BF16),
         pl.BlockSpec((1, A_HEADS, tm, A_HEAD_DIM), lambda i: (i // tps, 0, i % tps, 0))),
        (jax.ShapeDtypeStruct((nb, LANES, seq_len), BF16),
         pl.BlockSpec((1, LANES, tm), lambda i: (i // tps, 0, i % tps))),
        rows2(LANES, BF16),
        rows2(kw, BF16),
        (jax.ShapeDtypeStruct((nb, kw, seq_len), BF16),
         pl.BlockSpec((1, kw, tm), lambda i: (i // tps, 0, i % tps))),
        (jax.ShapeDtypeStruct((nb, B_HEADS, seq_len, B_V_DIM), BF16),
         pl.BlockSpec((1, B_HEADS, tm, B_V_DIM), lambda i: (i // tps, 0, i % tps, 0))),
        rows2(C_WIDTH, F32), rows2(C_WIDTH, F32), rows2(N_BRANCH * D_MODEL, F32),
    ]
    if not latent:
        outs += [rows2(LANES, F32), rows2(LANES, F32), rows2(B_KV_LORA, F32), rows2(B_ROPE, F32)]
    return pl.pallas_call(
        functools.partial(_inproj_kernel, latent=latent),
        grid=(t // tm,),
        in_specs=in_specs,
        out_specs=[o[1] for o in outs],
        out_shape=[o[0] for o in outs],
        compiler_params=_cparams("arbitrary"),
        name="inproj" + sfx,
    )(*args)


def _mla_cache_kernel(ckv_ref, kr_ref, wukt_ref, wuv_ref, placet_ref, kt_ref, v_ref):
    ckv = ckv_ref[0, 0].astype(BF16)
    knt = _dot_nt(wukt_ref[0], ckv)
    krt = _dot_nt(placet_ref[...], kr_ref[0, 0].astype(BF16))
    kt_ref[0, 0] = jnp.concatenate(
        [knt[j * HEAD_PAD:(j + 1) * HEAD_PAD, :] + krt for j in range(B_HEADS)], axis=0).astype(BF16)
    vv = _dot(ckv, wuv_ref[0])
    for hd in range(B_HEADS):
        v_ref[0, 0, hd] = vv[:, hd * B_V_DIM:(hd + 1) * B_V_DIM].astype(BF16)


def _mla_cache_call(cache_ckv, cache_krope, wukt, wuv, placet):
    nb, _, p, _ = cache_ckv.shape
    kw, vw = B_HEADS * HEAD_PAD, B_HEADS * B_V_DIM
    return pl.pallas_call(
        _mla_cache_kernel,
        grid=(DEPTH, nb),
        in_specs=[
            pl.BlockSpec((1, 1, p, B_KV_LORA), lambda l, b: (b, l, 0, 0)),
            pl.BlockSpec((1, 1, p, B_ROPE), lambda l, b: (b, l, 0, 0)),
            pl.BlockSpec((1, kw, B_KV_LORA), lambda l, b: (l, 0, 0)),
            pl.BlockSpec((1, B_KV_LORA, vw), lambda l, b: (l, 0, 0)),
            pl.BlockSpec((HEAD_PAD, B_ROPE), lambda l, b: (0, 0)),
        ],
        out_specs=[
            pl.BlockSpec((1, 1, kw, p), lambda l, b: (b, l, 0, 0)),
            pl.BlockSpec((1, 1, B_HEADS, p, B_V_DIM), lambda l, b: (b, l, 0, 0, 0)),
        ],
        out_shape=[jax.ShapeDtypeStruct((nb, DEPTH, kw, p), BF16),
                   jax.ShapeDtypeStruct((nb, DEPTH, B_HEADS, p, B_V_DIM), BF16)],
        compiler_params=_cparams("arbitrary", "arbitrary"),
        name="mla_cache_kv",
    )(cache_ckv, cache_krope, wukt, wuv, placet)


def _gqa_kernel(*refs, latent, seq_len):
    if latent:
        sink_ref, q_ref, kt_ref, v_ref, kct_ref, vc_ref, o_ref = refs
    else:
        sink_ref, q_ref, kt_ref, v_ref, o_ref = refs
    nq = q_ref.shape[2]
    rows = A_GROUP * nq
    if latent:
        win = 3 * Q_BLOCK
        i = pl.program_id(1)
        start = pl.multiple_of(jnp.clip((i - 1) * Q_BLOCK, 0, seq_len - win), Q_BLOCK)
        ktl = kt_ref[0, :, pl.ds(start, win)]
        vl = v_ref[0, pl.ds(start, win), :]
        qpos = i * Q_BLOCK + (lax.broadcasted_iota(jnp.int32, (rows, win), 0) & (nq - 1))
        kpos = start + lax.broadcasted_iota(jnp.int32, (rows, win), 1)
        valid = jnp.abs(qpos - kpos) <= A_WINDOW
        ktc = kct_ref[0, 0]
        vc = vc_ref[0, 0]
    else:
        ktl = kt_ref[0]
        vl = v_ref[0]
    outs = []
    for g in range(A_KV_HEADS):
        ks = slice(g * A_HEAD_DIM, (g + 1) * A_HEAD_DIM)
        qg = q_ref[0, g * A_GROUP:(g + 1) * A_GROUP].reshape(rows, A_HEAD_DIM)
        s_l = _dot(qg, ktl[ks, :])
        if latent:
            s_l = jnp.where(valid, s_l, NEG_INF)
            s_c = _dot(qg, ktc[ks, :])
        p_l, p_c, dens = [], [], []
        for j in range(A_GROUP):
            rs = slice(j * nq, (j + 1) * nq)
            sk = sink_ref[g * A_GROUP + j] * LOG2E
            m = jnp.maximum(jnp.max(s_l[rs], axis=-1, keepdims=True), sk)
            if latent:
                m = jnp.maximum(m, jnp.max(s_c[rs], axis=-1, keepdims=True))
            e_l = jnp.exp2(s_l[rs] - m)
            den = jnp.sum(e_l, axis=-1, keepdims=True) + jnp.exp2(sk - m)
            p_l.append(e_l.astype(BF16))
            if latent:
                e_c = jnp.exp2(s_c[rs] - m)
                den = den + jnp.sum(e_c, axis=-1, keepdims=True)
                p_c.append(e_c.astype(BF16))
            dens.append(den)
        o = _dot(jnp.concatenate(p_l, axis=0), vl[:, ks])
        if latent:
            o = o + _dot(jnp.concatenate(p_c, axis=0), vc[:, ks])
        for j in range(A_GROUP):
            outs.append(o[j * nq:(j + 1) * nq] / dens[j])
    o_ref[0] = jnp.concatenate(outs, axis=1).astype(BF16)


def _gqa_call(sink_l, q, kt, v, kct=None, vc=None, layer=0):
    nb, _, n, _ = q.shape
    latent = kct is not None
    qw = A_HEADS * A_HEAD_DIM
    smem = pl.BlockSpec(memory_space=pltpu.SMEM)
    whole_kt = pl.BlockSpec((1, LANES, n), lambda b, i: (b, 0, 0))
    whole_v = pl.BlockSpec((1, n, LANES), lambda b, i: (b, 0, 0))
    if latent:
        p = vc.shape[2]
        nq = Q_BLOCK
        in_specs = [smem,
                    pl.BlockSpec((1, A_HEADS, nq, A_HEAD_DIM), lambda b, i: (b, 0, i, 0)),
                    whole_kt, whole_v,
                    pl.BlockSpec((1, 1, LANES, p), lambda b, i: (b, layer, 0, 0)),
                    pl.BlockSpec((1, 1, p, LANES), lambda b, i: (b, layer, 0, 0))]
        args = (sink_l, q, kt, v, kct, vc)
    else:
        nq = n
        in_specs = [smem,
                    pl.BlockSpec((1, A_HEADS, nq, A_HEAD_DIM), lambda b, i: (b, 0, i, 0)),
                    whole_kt, whole_v]
        args = (sink_l, q, kt, v)
    return pl.pallas_call(
        functools.partial(_gqa_kernel, latent=latent, seq_len=n),
        grid=(nb, n // nq),
        in_specs=in_specs,
        out_specs=pl.BlockSpec((1, nq, qw), lambda b, i: (b, i, 0)),
        out_shape=jax.ShapeDtypeStruct((nb, n, qw), BF16),
        compiler_params=_cparams("arbitrary", "arbitrary"),
        name="gqa_lat" if latent else "gqa_ctx",
    )(*args)


def _mla_kernel(q_ref, kt_ref, v_ref, o_ref, m_sc, l_sc, acc_sc, s0, s1, p0, p1, a0, a1, *, n_chunks, tk):
    s_buf, p_buf, a_buf = (s0, s1), (p0, p1), (a0, a1)
    m_sc[...] = jnp.full(m_sc.shape, -jnp.inf, F32)
    l_sc[...] = jnp.zeros(l_sc.shape, F32)
    acc_sc[...] = jnp.zeros(acc_sc.shape, F32)

    def scores(c, hd):
        off = pl.multiple_of(c * tk, tk)
        rows = slice(hd * HEAD_PAD, (hd + 1) * HEAD_PAD)
        s_buf[hd % 2][...] = _dot(q_ref[0, :, rows], kt_ref[0, rows, pl.ds(off, tk)])

    def softmax(hd):
        s = s_buf[hd % 2][...]
        m_old = m_sc[hd]
        m_new = jnp.maximum(m_old, jnp.max(s, axis=-1, keepdims=True))
        alpha = jnp.exp2(m_old - m_new)
        p = jnp.exp2(s - m_new)
        l_sc[hd] = alpha * l_sc[hd] + jnp.sum(p, axis=-1, keepdims=True)
        m_sc[hd] = m_new
        p_buf[hd % 2][...] = p.astype(BF16)
        a_buf[hd % 2][...] = alpha

    def weighted_values(c, hd):
        off = pl.multiple_of(c * tk, tk)
        pv = _dot(p_buf[hd % 2][...], v_ref[0, hd, pl.ds(off, tk), :])
        acc_sc[hd] = a_buf[hd % 2][...] * acc_sc[hd] + pv

    def chunk_steps(c, last):
        for hd in range(B_HEADS):
            if hd + 2 < B_HEADS:
                scores(c, hd + 2)
            elif not last:
                scores(c + 1, hd + 2 - B_HEADS)
            if hd + 1 < B_HEADS:
                softmax(hd + 1)
            elif not last:
                softmax(0)
            weighted_values(c, hd)

    scores(0, 0)
    scores(0, 1)
    softmax(0)

    def body(c, carry):
        chunk_steps(c, False)
        return carry

    lax.fori_loop(0, n_chunks - 1, body, 0)
    chunk_steps(n_chunks - 1, True)
    o_ref[0] = jnp.concatenate(
        [acc_sc[hd] / l_sc[hd] for hd in range(B_HEADS)], axis=1).astype(BF16)


def _mla_call(q, kt, v, *, tq, tk, name):
    nb, kw, nk = kt.shape
    n = q.shape[1]
    vw = B_HEADS * B_V_DIM
    return pl.pallas_call(
        functools.partial(_mla_kernel, n_chunks=nk // tk, tk=tk),
        grid=(nb, n // tq),
        in_specs=[pl.BlockSpec((1, tq, kw), lambda b, i: (b, i, 0)),
                  pl.BlockSpec((1, kw, nk), lambda b, i: (b, 0, 0)),
                  pl.BlockSpec((1, B_HEADS, nk, B_V_DIM), lambda b, i: (b, 0, 0, 0))],
        out_specs=pl.BlockSpec((1, tq, vw), lambda b, i: (b, i, 0)),
        out_shape=jax.ShapeDtypeStruct((nb, n, vw), BF16),
        scratch_shapes=[pltpu.VMEM((B_HEADS, tq, 1), F32), pltpu.VMEM((B_HEADS, tq, 1), F32),
                        pltpu.VMEM((B_HEADS, tq, B_V_DIM), F32),
                        pltpu.VMEM((tq, tk), F32), pltpu.VMEM((tq, tk), F32),
                        pltpu.VMEM((tq, tk), BF16), pltpu.VMEM((tq, tk), BF16),
                        pltpu.VMEM((tq, 1), F32), pltpu.VMEM((tq, 1), F32)],
        compiler_params=_cparams("arbitrary", "arbitrary"),
        name=name,
    )(q, kt, v)


def _chunk_pitch(lc):
    return lc if (lc // SUBLANES) % 2 == 1 else lc + SUBLANES


def _lru_kernel(xc_ref, gg_ref, wconv_ref, bconv_ref, lam_ref, wg_ref, bg_ref, h0_ref,
                y_ref, st_ref, xpad, af, uf, ab, ub, *, seq_len, rows, n_seq):
    n = seq_len
    lc = n // SUBLANES
    pitch = _chunk_pitch(lc)
    pad = SUBLANES
    wconv = wconv_ref[...]
    bconv = bconv_ref[...]
    z = -lam_ref[...]
    softplus = jnp.maximum(z, 0.0) + jnp.log1p(jnp.exp(-jnp.abs(z)))
    k1 = (0.5 * C_RG) * softplus
    wg = wg_ref[0]
    bg = bg_ref[0]
    tiny = float(jnp.finfo(F32).tiny)
    xpad[0:pad, :] = jnp.zeros((pad, LANES), F32)
    xpad[pad + n:pad + n + pad, :] = jnp.zeros((pad, LANES), F32)

    def one_sequence(bi):
        xpad[pad:pad + n, :] = xc_ref[bi]
        for r0 in range(0, n, rows):
            dst = (r0 // lc) * pitch + r0 % lc
            xconv = bconv
            for j in range(C_CONV):
                xconv = xconv + xpad[pad + r0 + j - 1:pad + r0 + j - 1 + rows, :] * wconv[j:j + 1, :]
            g = _dot(xconv.astype(BF16), wg) + bg
            xh = 0.5 * xconv
            for d, (a_ref, u_ref) in enumerate(((af, uf), (ab, ub))):
                t_r = jnp.tanh(g[:, (2 * d) * LANES:(2 * d + 1) * LANES])
                t_i = jnp.tanh(g[:, (2 * d + 1) * LANES:(2 * d + 2) * LANES])
                kd = k1[d:d + 1, :]
                w = kd * t_r + kd
                a = jnp.exp2(w * (-LOG2E))
                zz = jnp.tanh(w) * (1.0 + a * a)
                mult = zz * lax.rsqrt(jnp.maximum(zz, tiny))
                a_ref[dst:dst + rows, :] = a
                u_ref[dst:dst + rows, :] = mult * (t_i * xh + xh)

        def scan_body(j, carry):
            hf, pf, hb, pb = carry
            fi = pl.ds(j, SUBLANES, stride=pitch)
            ri = pl.ds(lc - 1 - j, SUBLANES, stride=pitch)
            a_f = af[fi, :]
            a_b = ab[ri, :]
            hf = a_f * hf + uf[fi, :]
            hb = a_b * hb + ub[ri, :]
            pf = a_f * pf
            pb = a_b * pb
            uf[fi, :] = hf
            af[fi, :] = pf
            ub[ri, :] = hb
            ab[ri, :] = pb
            return hf, pf, hb, pb

        zeros = jnp.zeros((SUBLANES, LANES), F32)
        ones = jnp.ones((SUBLANES, LANES), F32)
        hf_end, af_end, hb_end, ab_end = lax.fori_loop(0, lc, scan_body, (zeros, ones, zeros, ones), unroll=8)

        h0 = h0_ref[bi]
        cf = [h0[0:1, :]]
        for s in range(SUBLANES):
            cf.append(af_end[s:s + 1, :] * cf[s] + hf_end[s:s + 1, :])
        cb = [None] * (SUBLANES + 1)
        cb[SUBLANES] = h0[1:2, :]
        for s in range(SUBLANES - 1, -1, -1):
            cb[s] = ab_end[s:s + 1, :] * cb[s + 1] + hb_end[s:s + 1, :]
        st_ref[bi] = jnp.concatenate([cf[SUBLANES], cb[0]], axis=0)

        for s in range(SUBLANES):
            src = slice(s * pitch, s * pitch + lc)
            rs = slice(s * lc, (s + 1) * lc)
            hf = uf[src, :] + af[src, :] * cf[s]
            hb = ub[src, :] + ab[src, :] * cb[s + 1]
            y_ref[bi, rs, :] = (gg_ref[bi, rs, :] * (hf + hb)).astype(BF16)

    if n_seq == 1:
        one_sequence(0)
    else:
        def seq_body(bi, carry):
            one_sequence(bi)
            return carry
        lax.fori_loop(0, n_seq, seq_body, 0)


def _lru_call(xc, gc, lw, h0):
    nb, n, _ = xc.shape
    ng = C_WIDTH // LANES
    lc = n // SUBLANES
    rows = min(lc, 512)
    n_scan = SUBLANES * _chunk_pitch(lc)
    bb = _pick_tile(nb, max(1, 1024 // n))
    seq = lambda b, g: (b, 0, g)
    return pl.pallas_call(
        functools.partial(_lru_kernel, seq_len=n, rows=rows, n_seq=bb),
        grid=(nb // bb, ng),
        in_specs=[
            pl.BlockSpec((bb, n, LANES), seq),
            pl.BlockSpec((bb, n, LANES), seq),
            pl.BlockSpec((C_CONV, LANES), lambda b, g: (0, g)),
            pl.BlockSpec((1, LANES), lambda b, g: (0, g)),
            pl.BlockSpec((2, LANES), lambda b, g: (0, g)),
            pl.BlockSpec((1, LANES, 4 * LANES), lambda b, g: (g, 0, 0)),
            pl.BlockSpec((1, 1, 4 * LANES), lambda b, g: (g, 0, 0)),
            pl.BlockSpec((bb, 2, LANES), seq),
        ],
        out_specs=[pl.BlockSpec((bb, n, LANES), seq), pl.BlockSpec((bb, 2, LANES), seq)],
        out_shape=[jax.ShapeDtypeStruct((nb, n, C_WIDTH), BF16),
                   jax.ShapeDtypeStruct((nb, 2, C_WIDTH), F32)],
        scratch_shapes=[pltpu.VMEM((n + 2 * SUBLANES, LANES), F32)] + [pltpu.VMEM((n_scan, LANES), F32)] * 4,
        compiler_params=_cparams("arbitrary", "arbitrary"),
        name="rglru",
    )(xc, gc, lw["w_conv"], lw["b_conv"], lw["lam"], lw["wgate"], lw["bgate"], h0)


def _merge_kernel(x_ref, ya_ref, yb_ref, yc_ref, gl_ref, g1_ref, gpost_ref, wbr_ref, wo_ref, o_ref):
    m = None
    for nbr, y_ref in enumerate((ya_ref, yb_ref, yc_ref)):
        gate = jax.nn.sigmoid(gl_ref[:, nbr * D_MODEL:(nbr + 1) * D_MODEL])
        term = gate * _dot(y_ref[...], wbr_ref[nbr])
        m = term if m is None else m + term
    o = _dot(m.astype(BF16), wo_ref[...])
    o_ref[...] = x_ref[...] + g1_ref[0] * _rms(o, gpost_ref[...])


def _merge_call(x, ya, yb, yc, gl, mod3, mod_row, lw, *, tm):
    t = x.shape[0]
    row = lambda i: (i, 0)
    return pl.pallas_call(
        _merge_kernel,
        grid=(t // tm,),
        in_specs=[
            pl.BlockSpec((tm, D_MODEL), row),
            pl.BlockSpec((tm, MIX_W), row),
            pl.BlockSpec((tm, MIX_W), row),
            pl.BlockSpec((tm, MIX_W), row),
            pl.BlockSpec((tm, N_BRANCH * D_MODEL), row),
            pl.BlockSpec((1, 1, D_MODEL), lambda i: (mod_row(i), 0, 2)),
            pl.BlockSpec((1, D_MODEL), lambda i: (0, 0)),
            pl.BlockSpec((N_BRANCH, MIX_W, D_MODEL), lambda i: (0, 0, 0)),
            pl.BlockSpec((D_MODEL, D_MODEL), lambda i: (0, 0)),
        ],
        out_specs=pl.BlockSpec((tm, D_MODEL), row),
        out_shape=jax.ShapeDtypeStruct((t, D_MODEL), F32),
        compiler_params=_cparams("arbitrary"),
        name="merge",
    )(x, ya, yb, yc, gl, mod3, lw["g_post1"], lw["w_branch"], lw["w_out"])


def _mlp_kernel(x_ref, sh_ref, sc_ref, g2_ref, gpre_ref, gpost_ref, w1_ref, w2_ref, o_ref, *, ff_chunk):
    x = x_ref[...]
    hb = (_rms(x, gpre_ref[...]) * (1.0 + sc_ref[0]) + sh_ref[0]).astype(BF16)
    f = None
    for c0 in range(0, D_FF, ff_chunk):
        u = jnp.maximum(_dot(hb, w1_ref[:, c0:c0 + ff_chunk]), 0.0)
        part = _dot((u * u).astype(BF16), w2_ref[c0:c0 + ff_chunk, :])
        f = part if f is None else f + part
    o_ref[...] = x + g2_ref[0] * _rms(f, gpost_ref[...])


def _mlp_call(x, mod3, mod_row, lw, *, tm):
    t = x.shape[0]
    row = lambda i: (i, 0)
    modspec = lambda col: pl.BlockSpec((1, 1, D_MODEL), lambda i: (mod_row(i), 0, col))
    return pl.pallas_call(
        functools.partial(_mlp_kernel, ff_chunk=1024),
        grid=(t // tm,),
        in_specs=[
            pl.BlockSpec((tm, D_MODEL), row),
            modspec(3), modspec(4), modspec(5),
            pl.BlockSpec((1, D_MODEL), lambda i: (0, 0)),
            pl.BlockSpec((1, D_MODEL), lambda i: (0, 0)),
            pl.BlockSpec((D_MODEL, D_FF), lambda i: (0, 0)),
            pl.BlockSpec((D_FF, D_MODEL), lambda i: (0, 0)),
        ],
        out_specs=pl.BlockSpec((tm, D_MODEL), row),
        out_shape=jax.ShapeDtypeStruct((t, D_MODEL), F32),
        compiler_params=_cparams("arbitrary"),
        name="mlp",
    )(x, mod3, mod3, mod3, lw["g_pre2"], lw["g_post2"], lw["w_ff1"], lw["w_ff2"])


def _rope_tables(n):
    pos = jnp.arange(n)
    rows = (pos // GRID_W).astype(F32)[:, None]
    cols = (pos % GRID_W).astype(F32)[:, None]

    def pattern(dim):
        half = dim // 4
        inv = jnp.power(ROPE_BASE, -jnp.arange(half, dtype=F32) * (2.0 / (dim // 2)))
        ar, ac = rows * inv, cols * inv
        c = jnp.concatenate([jnp.cos(ar), jnp.cos(ar), jnp.cos(ac), jnp.cos(ac)], axis=1)
        s = jnp.concatenate([-jnp.sin(ar), jnp.sin(ar), -jnp.sin(ac), jnp.sin(ac)], axis=1)
        return c, s

    c64, s64 = pattern(A_HEAD_DIM)
    ca = jnp.tile(c64, (1, LANES // A_HEAD_DIM))
    sa = jnp.tile(s64, (1, LANES // A_HEAD_DIM))
    c32, s32 = pattern(B_ROPE)
    tail = HEAD_PAD - B_QK_DIM
    cb = jnp.concatenate([jnp.ones((n, B_NOPE), F32), c32, jnp.ones((n, tail), F32)], axis=1)
    sb = jnp.concatenate([jnp.zeros((n, B_NOPE), F32), s32, jnp.zeros((n, tail), F32)], axis=1)
    return (ca * A_QSCALE, sa * A_QSCALE, cb * B_QSCALE, sb * B_QSCALE, ca.T, sa.T, cb.T, sb.T)


def _prep_weights(p):
    w_in = p["w_in"]
    qw = A_HEADS * A_HEAD_DIM
    o_k = qw
    o_v = o_k + LANES
    o_cq = o_v + LANES
    o_ckv = o_cq + B_Q_LORA
    o_kr = o_ckv + B_KV_LORA
    o_xc = o_kr + B_ROPE
    o_gl = o_xc + 2 * C_WIDTH
    w_q, w_k, w_v = w_in[:, :, :o_k], w_in[:, :, o_k:o_v], w_in[:, :, o_v:o_cq]
    w_cq, w_ckv, w_kr = w_in[:, :, o_cq:o_ckv], w_in[:, :, o_ckv:o_kr], w_in[:, :, o_kr:o_xc]
    kr_placed = jnp.pad(w_kr, ((0, 0), (0, 0), (B_NOPE, HEAD_PAD - B_QK_DIM)))
    pad_heads = lambda w, dh: jnp.pad(
        w.reshape(DEPTH, w.shape[1], B_HEADS, dh), ((0, 0), (0, 0), (0, 0), (0, HEAD_PAD - dh))
    ).reshape(DEPTH, w.shape[1], B_HEADS * HEAD_PAD)
    tr = lambda w: jnp.swapaxes(w, 1, 2)

    eye = jnp.eye(C_BLOCKS, dtype=F32)
    ng = C_WIDTH // LANES

    def gate_groups(w):
        full = (w[:, :, :, None, :] * eye[None, :, None, :, None]).reshape(DEPTH, C_WIDTH, C_WIDTH)
        full = full.reshape(DEPTH, ng, LANES, ng, LANES)
        return jnp.stack([full[:, g, :, g, :] for g in range(ng)], axis=1)

    w_ra, w_ri, b_ra, b_ri = p["w_ra"], p["w_ri"], p["b_ra"], p["b_ri"]
    wgate = jnp.concatenate([gate_groups(w_ra[:, 0]), gate_groups(w_ri[:, 0]),
                             gate_groups(w_ra[:, 1]), gate_groups(w_ri[:, 1])], axis=3)
    grp = lambda b: b.reshape(DEPTH, ng, 1, LANES)
    bgate = jnp.concatenate([grp(b_ra[:, 0]), grp(b_ri[:, 0]), grp(b_ra[:, 1]), grp(b_ri[:, 1])], axis=3)

    row = lambda g: g[:, None, :]
    return dict(
        g_pre1=row(p["g_pre1"]), g_post1=row(p["g_post1"]), g_pre2=row(p["g_pre2"]), g_post2=row(p["g_post2"]),
        g_q=row(p["g_q"]), g_kv=row(p["g_kv"]),
        wa_ctx=w_in[:, :, :o_cq].astype(BF16),
        wa_lat=jnp.concatenate([w_q, w_v], axis=2).astype(BF16),
        wkat=tr(w_k).astype(BF16),
        wb_ctx=jnp.concatenate([w_cq, kr_placed, w_ckv], axis=2).astype(BF16),
        wb_lat=jnp.concatenate([w_cq, w_ckv], axis=2).astype(BF16),
        wkrt=tr(kr_placed).astype(BF16),
        wl=w_in[:, :, o_xc:o_gl].astype(BF16), wg=w_in[:, :, o_gl:].astype(BF16),
        wq=pad_heads(p["w_q_up"], B_QK_DIM).astype(BF16),
        wukt=tr(pad_heads(p["w_uk"], B_NOPE)).astype(BF16),
        wuv=p["w_uv"].astype(BF16),
        w_conv=p["w_conv"], b_conv=row(p["b_conv"]), lam=p["lam"],
        wgate=(0.5 * wgate).astype(BF16), bgate=0.5 * bgate,
        w_branch=p["w_branch"].astype(BF16), w_out=p["w_out"].astype(BF16),
        w_ff1=p["w_ff1"].astype(BF16), w_ff2=p["w_ff2"].astype(BF16),
    )


def _pick_tile(n, target, multiple=1):
    t = max(multiple, min(n, target) // multiple * multiple)
    while n % t:
        t -= multiple
    return t


def kernel(x_prompt, x_sample, cache_attn_k, cache_attn_v, cache_mla_ckv, cache_mla_krope, state_lru, c, c_ctx, w_ada, b_ada, g_pre1, g_post1, g_pre2, g_post2, w_in, sink, g_q, w_q_up, g_kv, w_uk, w_uv, w_conv, b_conv, lam, w_ra, b_ra, w_ri, b_ri, w_branch, w_out, w_ff1, w_ff2):
    nbc, seq, _ = x_prompt.shape
    nbl, n_lat, _ = x_sample.shape
    past = cache_attn_k.shape[2]
    assert n_lat % GRID_W == 0 and n_lat >= 3 * Q_BLOCK and n_lat % Q_BLOCK == 0
    assert seq % (SUBLANES * SUBLANES) == 0 and n_lat % (SUBLANES * SUBLANES) == 0
    assert seq & (seq - 1) == 0, "context GQA stacks heads on rows with a power-of-two row count"

    weights = _prep_weights(dict(
        g_pre1=g_pre1, g_post1=g_post1, g_pre2=g_pre2, g_post2=g_post2, w_in=w_in, g_q=g_q,
        w_q_up=w_q_up, g_kv=g_kv, w_uk=w_uk, w_uv=w_uv, w_conv=w_conv, b_conv=b_conv, lam=lam,
        w_ra=w_ra, b_ra=b_ra, w_ri=w_ri, b_ri=b_ri, w_branch=w_branch, w_out=w_out,
        w_ff1=w_ff1, w_ff2=w_ff2))
    tables = _rope_tables(n_lat)

    n_mod = -(-(nbl + 1) // SUBLANES) * SUBLANES
    cmat = jnp.zeros((n_mod, D_MODEL), F32).at[:nbl].set(c).at[nbl].set(c_ctx)
    mod3 = _ada_call(cmat, w_ada, b_ada).reshape(DEPTH * n_mod, 1, N_ADA * D_MODEL)

    placet = jnp.pad(jnp.eye(B_ROPE, dtype=BF16), ((B_NOPE, HEAD_PAD - B_QK_DIM), (0, 0)))
    kct_mla, vc_mla = _mla_cache_call(cache_mla_ckv, cache_mla_krope, weights["wukt"], weights["wuv"], placet)
    kct_a = jnp.swapaxes(cache_attn_k.reshape(nbl, DEPTH, past, LANES), 2, 3).astype(BF16)
    vc_a = cache_attn_v.reshape(nbl, DEPTH, past, LANES).astype(BF16)

    tm_p = _pick_tile(seq, 256)
    tm_s = _pick_tile(n_lat, 256)
    y_p = x_prompt.reshape(nbc * seq, D_MODEL)
    y_s = x_sample.reshape(nbl * n_lat, D_MODEL)
    zero_state = jnp.zeros((nbc, 2, C_WIDTH), F32)
    ks_l, vs_l, ckv_l, kr_l, st_l = [], [], [], [], []
    for l in range(DEPTH):
        lw = {k: v[l] for k, v in weights.items()}
        row_p = lambda i, l=l: l * n_mod + nbl
        row_s = lambda i, l=l: l * n_mod + (i * tm_s) // n_lat

        (qa, kat, va, qb, kbt, vb, xc, gc, gl, kaf, vaf, ckvn, krf) = _inproj_call(
            y_p, mod3, row_p, lw, None, latent=False, seq_len=seq, tm=tm_p)
        r3 = lambda a: a.reshape(nbc, seq, a.shape[-1])
        ya = _gqa_call(sink[l], qa, kat, r3(va))
        yb = _mla_call(r3(qb), kbt, vb, tq=seq, tk=seq, name="mla_ctx")
        yc, st = _lru_call(r3(xc), r3(gc), lw, zero_state)
        f2 = lambda a: a.reshape(nbc * seq, a.shape[-1])
        y_p = _merge_call(y_p, f2(ya), f2(yb), f2(yc), gl, mod3, row_p, lw, tm=tm_p)
        y_p = _mlp_call(y_p, mod3, row_p, lw, tm=tm_p)
        ks_l.append(kaf.reshape(nbc, seq, A_KV_HEADS, A_HEAD_DIM))
        vs_l.append(vaf.reshape(nbc, seq, A_KV_HEADS, A_HEAD_DIM))
        ckv_l.append(ckvn.reshape(nbc, seq, B_KV_LORA))
        kr_l.append(krf.reshape(nbc, seq, B_ROPE))
        st_l.append(st)

        (qa, kat, va, qb, kbt, vb, xc, gc, gl) = _inproj_call(
            y_s, mod3, row_s, lw, tables, latent=True, seq_len=n_lat, tm=tm_s)
        r3 = lambda a: a.reshape(nbl, n_lat, a.shape[-1])
        ya = _gqa_call(sink[l], qa, kat, r3(va), kct_a, vc_a, layer=l)
        kt_all = jnp.concatenate([kbt, kct_mla[:, l]], axis=2)
        v_all = jnp.concatenate([vb, vc_mla[:, l]], axis=2)
        yb = _mla_call(r3(qb), kt_all, v_all, tq=_pick_tile(n_lat, 256),
                       tk=_pick_tile(n_lat + past, 512, LANES), name="mla_lat")
        yc, _ = _lru_call(r3(xc), r3(gc), lw, state_lru[:, l])
        f2 = lambda a: a.reshape(nbl * n_lat, a.shape[-1])
        y_s = _merge_call(y_s, f2(ya), f2(yb), f2(yc), gl, mod3, row_s, lw, tm=tm_s)
        y_s = _mlp_call(y_s, mod3, row_s, lw, tm=tm_s)

    return (y_p.reshape(nbc, seq, D_MODEL), y_s.reshape(nbl, n_lat, D_MODEL),
            jnp.stack(ks_l, axis=1), jnp.stack(vs_l, axis=1), jnp.stack(ckv_l, axis=1),
            jnp.stack(kr_l, axis=1), jnp.stack(st_l, axis=1))
```

```python
import functools
import math

import jax
import jax.numpy as jnp
from jax import lax
from jax.experimental import pallas as pl
from jax.experimental.pallas import tpu as pltpu

F32 = jnp.float32
BF16 = jnp.bfloat16

D_MODEL = 1024
DEPTH = 4
GRID_W = 64
Q_BLOCK = 128
A_HEADS = 8
A_KV_HEADS = 2
A_HEAD_DIM = 64
A_GROUP = A_HEADS // A_KV_HEADS
A_WINDOW = 128
B_HEADS = 8
B_Q_LORA = 384
B_KV_LORA = 256
B_NOPE = 64
B_ROPE = 32
B_QK_DIM = B_NOPE + B_ROPE
B_V_DIM = 64
C_WIDTH = 512
C_BLOCKS = 8
C_BLOCK_W = C_WIDTH // C_BLOCKS
C_CONV = 4
C_RG = 8.0
MIX_W = 512
N_BRANCH = 3
D_FF = 4 * D_MODEL
N_ADA = 6
ROPE_BASE = 10000.0
EPS = 1e-6
NEG_INF = -1e30
LOG2E = math.log2(math.e)

LANES = 128
SUBLANES = 8
HEAD_PAD = 128
V_ROWS = 80
VMEM_LIMIT = 56 * 1024 * 1024
A_QSCALE = A_HEAD_DIM ** -0.5 * LOG2E
B_QSCALE = B_QK_DIM ** -0.5 * LOG2E


def _cparams(*sem):
    return pltpu.CompilerParams(dimension_semantics=sem, vmem_limit_bytes=VMEM_LIMIT)


def _dot(a, b):
    return jnp.dot(a, b, preferred_element_type=F32)


def _dot_nt(a, b):
    return lax.dot_general(a, b, (((1,), (1,)), ((), ())), preferred_element_type=F32)


def _rms(x, g):
    return x * lax.rsqrt(jnp.mean(x * x, axis=-1, keepdims=True) + EPS) * g


def _rope(x, c, s, half, axis):
    pos = lax.broadcasted_iota(jnp.int32, x.shape, axis)
    lo = (pos & (2 * half - 1)) < half
    partner = jnp.where(lo, pltpu.roll(x, LANES - half, axis), pltpu.roll(x, half, axis))
    return x * c + partner * s


def _ada_kernel(c_ref, w_ref, b_ref, o_ref):
    c = c_ref[...]
    a = (c * jax.nn.sigmoid(c)).astype(BF16)
    o_ref[0] = _dot(a, w_ref[0].astype(BF16)) + b_ref[0]


def _ada_call(cmat, w_ada, b_ada):
    nb = cmat.shape[0]
    n_out = N_ADA * D_MODEL
    tn = 1536
    return pl.pallas_call(
        _ada_kernel,
        grid=(DEPTH, n_out // tn),
        in_specs=[
            pl.BlockSpec((nb, D_MODEL), lambda l, j: (0, 0)),
            pl.BlockSpec((1, D_MODEL, tn), lambda l, j: (l, 0, j)),
            pl.BlockSpec((1, 1, tn), lambda l, j: (l, 0, j)),
        ],
        out_specs=pl.BlockSpec((1, nb, tn), lambda l, j: (l, 0, j)),
        out_shape=jax.ShapeDtypeStruct((DEPTH, nb, n_out), F32),
        compiler_params=_cparams("arbitrary", "arbitrary"),
        name="ada_mod",
    )(cmat, w_ada, b_ada.reshape(DEPTH, 1, n_out))


def _ones_row_block(cols):
    r = lax.broadcasted_iota(jnp.int32, (V_ROWS - B_V_DIM, cols), 0)
    return jnp.where(r == 0, 1.0, 0.0).astype(BF16)


def _inproj_kernel(*refs, latent):
    (x_ref, sh_ref, sc_ref, gpre_ref, wqat_ref, wkv_ref, wvat_ref, wb_ref, wl_ref, wg_ref,
     gq_ref, wqt_ref, gkv_ref, wuk_ref, wuvt_ref) = refs[:15]
    refs = refs[15:]
    if latent:
        ca_ref, sa_ref, cb_ref, sb_ref, caqt_ref, saqt_ref, cbqt_ref, sbqt_ref = refs[:8]
        refs = refs[8:]
        qat_ref, ka_ref, vat_ref, qbt_ref, kb_ref, vbt_ref, xc_ref, gc_ref, gl_ref = refs
    else:
        (qat_ref, ka_ref, vat_ref, qbt_ref, kb_ref, vbt_ref, xc_ref, gc_ref, gl_ref,
         kaf_ref, vaf_ref, ckvn_ref, krf_ref) = refs

    x = x_ref[...]
    h = _rms(x, gpre_ref[...]) * (1.0 + sc_ref[0]) + sh_ref[0]
    hb = h.astype(BF16)

    tm = x.shape[0]
    ones_blk = _ones_row_block(tm)
    qat = _dot_nt(wqat_ref[...], hb)
    pkv = _dot(hb, wkv_ref[...])
    ka = pkv[:, :LANES]
    vat = _dot_nt(wvat_ref[...], hb)
    if latent:
        caqt, saqt = caqt_ref[...], saqt_ref[...]
        qat = jnp.concatenate(
            [_rope(qat[j * LANES:(j + 1) * LANES, :], caqt, saqt, 16, 0)
             for j in range(A_HEADS * A_HEAD_DIM // LANES)], axis=0)
        ka = _rope(ka, ca_ref[...], sa_ref[...], 16, 1)
    else:
        qat = qat * A_QSCALE
        kaf_ref[...] = ka
        vaf_ref[...] = pkv[:, LANES:]
    qat_ref[0] = qat.astype(BF16)
    for g in range(A_KV_HEADS):
        ka_ref[0, g] = ka[:, g * A_HEAD_DIM:(g + 1) * A_HEAD_DIM].astype(BF16)
        vat_ref[0, g, 0:A_HEAD_DIM, :] = vat[g * A_HEAD_DIM:(g + 1) * A_HEAD_DIM, :].astype(BF16)
        vat_ref[0, g, A_HEAD_DIM:V_ROWS, :] = ones_blk

    pb = _dot(hb, wb_ref[...])
    cq = pb[:, :B_Q_LORA]
    krp = pb[:, B_Q_LORA:B_Q_LORA + HEAD_PAD]
    ckv = pb[:, B_Q_LORA + HEAD_PAD:]
    qt = _dot_nt(wqt_ref[...], _rms(cq, gq_ref[...]).astype(BF16))
    ckvn = _rms(ckv, gkv_ref[...])
    ckvb = ckvn.astype(BF16)
    kn = _dot(ckvb, wuk_ref[...])
    vt = _dot_nt(wuvt_ref[...], ckvb)
    for hd in range(B_HEADS):
        vbt_ref[0, hd, 0:B_V_DIM, :] = vt[hd * B_V_DIM:(hd + 1) * B_V_DIM, :].astype(BF16)
        vbt_ref[0, hd, B_V_DIM:V_ROWS, :] = ones_blk
    if latent:
        cbqt, sbqt = cbqt_ref[...], sbqt_ref[...]
        qt = jnp.concatenate(
            [_rope(qt[j * HEAD_PAD:(j + 1) * HEAD_PAD, :], cbqt, sbqt, 8, 0) for j in range(B_HEADS)], axis=0)
        krp = _rope(krp, cb_ref[...], sb_ref[...], 8, 1)
    else:
        qt = qt * B_QSCALE
        ckvn_ref[...] = ckvn
        krf_ref[...] = krp[:, B_NOPE:B_QK_DIM]
    qbt_ref[0] = qt.astype(BF16)
    kb_ref[...] = jnp.concatenate(
        [kn[:, j * HEAD_PAD:(j + 1) * HEAD_PAD] + krp for j in range(B_HEADS)], axis=1).astype(BF16)

    pc = _dot(hb, wl_ref[...])
    xc_ref[...] = pc[:, :C_WIDTH]
    gc_ref[...] = jax.nn.gelu(pc[:, C_WIDTH:])
    gl_ref[...] = _dot(hb, wg_ref[...])


def _inproj_call(x, mod3, mod_row, lw, tables, *, latent, seq_len, tm):
    t = x.shape[0]
    nb = t // seq_len
    tps = seq_len // tm
    row = lambda i: (i, 0)
    const = lambda i: (0, 0)
    modspec = lambda col: pl.BlockSpec((1, 1, D_MODEL), lambda i: (mod_row(i), 0, col))
    wspec = lambda w: pl.BlockSpec(w.shape, const)
    sfx = "_lat" if latent else "_ctx"
    weights = [lw["g_pre1"], lw["wqat"], lw["wkv"], lw["wvat"], lw["wb"], lw["wl"], lw["wg"],
               lw["g_q"], lw["wqt"], lw["g_kv"], lw["wuk"], lw["wuvt"]]
    in_specs = [pl.BlockSpec((tm, D_MODEL), row), modspec(0), modspec(1)] + [wspec(w) for w in weights]
    args = [x, mod3, mod3] + weights
    if latent:
        rspec = pl.BlockSpec((tm, LANES), lambda i: (i % tps, 0))
        cspec = pl.BlockSpec((LANES, tm), lambda i: (0, i % tps))
        in_specs += [rspec] * 4 + [cspec] * 4
        args += list(tables)
    rows2 = lambda w, dt: (jax.ShapeDtypeStruct((t, w), dt), pl.BlockSpec((tm, w), row))
    kw = B_HEADS * HEAD_PAD
    qw = A_HEADS * A_HEAD_DIM
    outs = [
        (jax.ShapeDtypeStruct((nb, qw, seq_len), BF16),
         pl.BlockSpec((1, qw, tm), lambda i: (i // tps, 0, i % tps))),
        (jax.ShapeDtypeStruct((nb, A_KV_HEADS, seq_len, A_HEAD_DIM), BF16),
         pl.BlockSpec((1, A_KV_HEADS, tm, A_HEAD_DIM), lambda i: (i // tps, 0, i % tps, 0))),
        (jax.ShapeDtypeStruct((nb, A_KV_HEADS, V_ROWS, seq_len), BF16),
         pl.BlockSpec((1, A_KV_HEADS, V_ROWS, tm), lambda i: (i // tps, 0, 0, i % tps))),
        (jax.ShapeDtypeStruct((nb, kw, seq_len), BF16),
         pl.BlockSpec((1, kw, tm), lambda i: (i // tps, 0, i % tps))),
        rows2(kw, BF16),
        (jax.ShapeDtypeStruct((nb, B_HEADS, V_ROWS, seq_len), BF16),
         pl.BlockSpec((1, B_HEADS, V_ROWS, tm), lambda i: (i // tps, 0, 0, i % tps))),
        rows2(C_WIDTH, F32), rows2(C_WIDTH, F32), rows2(N_BRANCH * D_MODEL, F32),
    ]
    if not latent:
        outs += [rows2(LANES, F32), rows2(LANES, F32), rows2(B_KV_LORA, F32), rows2(B_ROPE, F32)]
    return pl.pallas_call(
        functools.partial(_inproj_kernel, latent=latent),
        grid=(t // tm,),
        in_specs=in_specs,
        out_specs=[o[1] for o in outs],
        out_shape=[o[0] for o in outs],
        compiler_params=_cparams("arbitrary"),
        name="inproj" + sfx,
    )(*args)


def _mla_cache_kernel(ckv_ref, kr_ref, wuk_ref, wuvt_ref, place_ref, k_ref, vt_ref):
    ckv = ckv_ref[0, 0].astype(BF16)
    kn = _dot(ckv, wuk_ref[0])
    krp = _dot(kr_ref[0, 0].astype(BF16), place_ref[...])
    k_ref[0, 0] = jnp.concatenate(
        [kn[:, j * HEAD_PAD:(j + 1) * HEAD_PAD] + krp for j in range(B_HEADS)], axis=1).astype(BF16)
    vt = _dot_nt(wuvt_ref[0], ckv)
    ones_blk = _ones_row_block(ckv.shape[0])
    for hd in range(B_HEADS):
        vt_ref[0, 0, hd, 0:B_V_DIM, :] = vt[hd * B_V_DIM:(hd + 1) * B_V_DIM, :].astype(BF16)
        vt_ref[0, 0, hd, B_V_DIM:V_ROWS, :] = ones_blk


def _mla_cache_call(cache_ckv, cache_krope, wuk, wuvt, place):
    nb, _, p, _ = cache_ckv.shape
    kw, vw = B_HEADS * HEAD_PAD, B_HEADS * B_V_DIM
    return pl.pallas_call(
        _mla_cache_kernel,
        grid=(DEPTH, nb),
        in_specs=[
            pl.BlockSpec((1, 1, p, B_KV_LORA), lambda l, b: (b, l, 0, 0)),
            pl.BlockSpec((1, 1, p, B_ROPE), lambda l, b: (b, l, 0, 0)),
            pl.BlockSpec((1, B_KV_LORA, kw), lambda l, b: (l, 0, 0)),
            pl.BlockSpec((1, vw, B_KV_LORA), lambda l, b: (l, 0, 0)),
            pl.BlockSpec((B_ROPE, HEAD_PAD), lambda l, b: (0, 0)),
        ],
        out_specs=[
            pl.BlockSpec((1, 1, p, kw), lambda l, b: (b, l, 0, 0)),
            pl.BlockSpec((1, 1, B_HEADS, V_ROWS, p), lambda l, b: (b, l, 0, 0, 0)),
        ],
        out_shape=[jax.ShapeDtypeStruct((nb, DEPTH, p, kw), BF16),
                   jax.ShapeDtypeStruct((nb, DEPTH, B_HEADS, V_ROWS, p), BF16)],
        compiler_params=_cparams("arbitrary", "arbitrary"),
        name="mla_cache_kv",
    )(cache_ckv, cache_krope, wuk, wuvt, place)


def _gqa_kernel(*refs, latent, seq_len):
    if latent:
        sink_ref, qt_ref, k_ref, vt_ref, kc_ref, vtc_ref, o_ref = refs
    else:
        sink_ref, qt_ref, k_ref, vt_ref, o_ref = refs
    nq = qt_ref.shape[2]
    cols = A_GROUP * nq
    if latent:
        win = 3 * Q_BLOCK
        i = pl.program_id(1)
        start = pl.multiple_of(jnp.clip((i - 1) * Q_BLOCK, 0, seq_len - win), Q_BLOCK)
        keys = pl.ds(start, win)
        kpos = start + lax.broadcasted_iota(jnp.int32, (win, cols), 0)
        qpos = i * Q_BLOCK + (lax.broadcasted_iota(jnp.int32, (win, cols), 1) & (nq - 1))
        valid = jnp.abs(qpos - kpos) <= A_WINDOW
    else:
        keys = slice(None)
    scores = []
    for g in range(A_KV_HEADS):
        qgt = jnp.concatenate(
            [qt_ref[0, hd * A_HEAD_DIM:(hd + 1) * A_HEAD_DIM, :]
             for hd in range(g * A_GROUP, (g + 1) * A_GROUP)], axis=1)
        s_l = _dot(k_ref[0, g, keys, :], qgt)
        s_c = _dot(kc_ref[0, 0, g], qgt) if latent else None
        scores.append((s_l, s_c))
    heads_t = []
    for g in range(A_KV_HEADS):
        s_l, s_c = scores[g]
        sink2 = jnp.concatenate(
            [jnp.full((1, nq), sink_ref[hd] * LOG2E, F32)
             for hd in range(g * A_GROUP, (g + 1) * A_GROUP)], axis=1)
        m = sink2
        if latent:
            s_l = jnp.where(valid, s_l, NEG_INF)
            m = jnp.maximum(m, jnp.max(s_c, axis=0, keepdims=True))
        m = jnp.maximum(m, jnp.max(s_l, axis=0, keepdims=True))
        acc = _dot(vt_ref[0, g, :, keys], jnp.exp2(s_l - m).astype(BF16))
        if latent:
            acc = acc + _dot(vtc_ref[0, 0, g], jnp.exp2(s_c - m).astype(BF16))
        den = acc[A_HEAD_DIM:A_HEAD_DIM + 1, :] + jnp.exp2(sink2 - m)
        o_t = acc[:A_HEAD_DIM, :] / den
        heads_t += [o_t[:, j * nq:(j + 1) * nq] for j in range(A_GROUP)]
    o_ref[0] = jnp.concatenate(heads_t, axis=0).T.astype(BF16)


def _gqa_call(sink_l, qt, k, vt, kc=None, vtc=None, layer=0):
    nb, qw, n = qt.shape
    latent = kc is not None
    smem = pl.BlockSpec(memory_space=pltpu.SMEM)
    nq = Q_BLOCK if latent else n
    in_specs = [smem,
                pl.BlockSpec((1, qw, nq), lambda b, i: (b, 0, i)),
                pl.BlockSpec((1, A_KV_HEADS, n, A_HEAD_DIM), lambda b, i: (b, 0, 0, 0)),
                pl.BlockSpec((1, A_KV_HEADS, V_ROWS, n), lambda b, i: (b, 0, 0, 0))]
    args = [sink_l, qt, k, vt]
    if latent:
        p = kc.shape[3]
        in_specs += [pl.BlockSpec((1, 1, A_KV_HEADS, p, A_HEAD_DIM), lambda b, i: (b, layer, 0, 0, 0)),
                     pl.BlockSpec((1, 1, A_KV_HEADS, V_ROWS, p), lambda b, i: (b, layer, 0, 0, 0))]
        args += [kc, vtc]
    return pl.pallas_call(
        functools.partial(_gqa_kernel, latent=latent, seq_len=n),
        grid=(nb, n // nq),
        in_specs=in_specs,
        out_specs=pl.BlockSpec((1, nq, qw), lambda b, i: (b, i, 0)),
        out_shape=jax.ShapeDtypeStruct((nb, n, qw), BF16),
        compiler_params=_cparams("arbitrary", "arbitrary"),
        name="gqa_lat" if latent else "gqa_ctx",
    )(*args)


def _mla_kernel(*refs, has_ctx, n_own, n_ctx, tk):
    if has_ctx:
        qt_ref, k_ref, vt_ref, kc_ref, vtc_ref, o_ref, m_sc, acc_sc, s0, s1, p0, p1, a0, a1 = refs
    else:
        qt_ref, k_ref, vt_ref, o_ref, m_sc, acc_sc, s0, s1, p0, p1, a0, a1 = refs
    s_buf, p_buf, a_buf = (s0, s1), (p0, p1), (a0, a1)
    m_sc[...] = jnp.full(m_sc.shape, -jnp.inf, F32)
    acc_sc[...] = jnp.zeros(acc_sc.shape, F32)

    def span(c):
        return pl.ds(c * tk, tk) if isinstance(c, int) else pl.ds(pl.multiple_of(c * tk, tk), tk)

    def scores(src, c, hd):
        cols = slice(hd * HEAD_PAD, (hd + 1) * HEAD_PAD)
        k = k_ref[0, span(c), cols] if src == "own" else kc_ref[0, 0, span(c), cols]
        s_buf[hd % 2][...] = _dot(k, qt_ref[0, cols, :])

    def softmax(hd):
        s = s_buf[hd % 2][...]
        m_old = m_sc[hd]
        m_new = jnp.maximum(m_old, jnp.max(s, axis=0, keepdims=True))
        p_buf[hd % 2][...] = jnp.exp2(s - m_new).astype(BF16)
        a_buf[hd % 2][...] = jnp.exp2(m_old - m_new)
        m_sc[hd] = m_new

    def weighted_values(src, c, hd):
        vt = vt_ref[0, hd, :, span(c)] if src == "own" else vtc_ref[0, 0, hd, :, span(c)]
        acc_sc[hd] = a_buf[hd % 2][...] * acc_sc[hd] + _dot(vt, p_buf[hd % 2][...])

    def chunk_steps(cur, nxt):
        for hd in range(B_HEADS):
            if hd + 2 < B_HEADS:
                scores(*cur, hd + 2)
            elif nxt is not None:
                scores(*nxt, hd + 2 - B_HEADS)
            if hd + 1 < B_HEADS:
                softmax(hd + 1)
            elif nxt is not None:
                softmax(0)
            weighted_values(*cur, hd)

    def run(src, count, after):
        def body(c, carry):
            chunk_steps((src, c), (src, c + 1))
            return carry
        lax.fori_loop(0, count - 1, body, 0)
        chunk_steps((src, count - 1), after)

    scores("own", 0, 0)
    scores("own", 0, 1)
    softmax(0)
    if has_ctx:
        run("own", n_own, ("ctx", 0))
        run("ctx", n_ctx, None)
    else:
        run("own", n_own, None)
    outs = []
    for hd in range(B_HEADS):
        acc = acc_sc[hd]
        outs.append(acc[:B_V_DIM, :] / acc[B_V_DIM:B_V_DIM + 1, :])
    o_ref[0] = jnp.concatenate(outs, axis=0).T.astype(BF16)


def _mla_call(qt, k, vt, kc=None, vtc=None, layer=0, *, tq, tk):
    nb, n, kw = k.shape
    vw = B_HEADS * B_V_DIM
    has_ctx = kc is not None
    in_specs = [pl.BlockSpec((1, kw, tq), lambda b, i: (b, 0, i)),
                pl.BlockSpec((1, n, kw), lambda b, i: (b, 0, 0)),
                pl.BlockSpec((1, B_HEADS, V_ROWS, n), lambda b, i: (b, 0, 0, 0))]
    args = [qt, k, vt]
    n_ctx = 0
    if has_ctx:
        p = kc.shape[2]
        n_ctx = p // tk
        in_specs += [pl.BlockSpec((1, 1, p, kw), lambda b, i: (b, layer, 0, 0)),
                     pl.BlockSpec((1, 1, B_HEADS, V_ROWS, p), lambda b, i: (b, layer, 0, 0, 0))]
        args += [kc, vtc]
    return pl.pallas_call(
        functools.partial(_mla_kernel, has_ctx=has_ctx, n_own=n // tk, n_ctx=n_ctx, tk=tk),
        grid=(nb, n // tq),
        in_specs=in_specs,
        out_specs=pl.BlockSpec((1, tq, vw), lambda b, i: (b, i, 0)),
        out_shape=jax.ShapeDtypeStruct((nb, n, vw), BF16),
        scratch_shapes=[pltpu.VMEM((B_HEADS, 1, tq), F32), pltpu.VMEM((B_HEADS, V_ROWS, tq), F32),
                        pltpu.VMEM((tk, tq), F32), pltpu.VMEM((tk, tq), F32),
                        pltpu.VMEM((tk, tq), BF16), pltpu.VMEM((tk, tq), BF16),
                        pltpu.VMEM((1, tq), F32), pltpu.VMEM((1, tq), F32)],
        compiler_params=_cparams("arbitrary", "arbitrary"),
        name="mla_lat" if has_ctx else "mla_ctx",
    )(*args)


def _chunk_pitch(lc):
    return lc if (lc // SUBLANES) % 2 == 1 else lc + SUBLANES


def _lru_kernel(xc_ref, gg_ref, wconv_ref, bconv_ref, lam_ref, wg_ref, bg_ref, h0_ref,
                y_ref, st_ref, xpad, af, uf, ab, ub, *, seq_len, rows, n_seq):
    n = seq_len
    lc = n // SUBLANES
    pitch = _chunk_pitch(lc)
    pad = SUBLANES
    wconv = wconv_ref[...]
    bconv = bconv_ref[...]
    z = -lam_ref[...]
    softplus = jnp.maximum(z, 0.0) + jnp.log1p(jnp.exp(-jnp.abs(z)))
    k1 = (0.5 * C_RG) * softplus
    wg = wg_ref[0]
    bg = bg_ref[0]
    tiny = float(jnp.finfo(F32).tiny)
    xpad[0:pad, :] = jnp.zeros((pad, LANES), F32)
    xpad[pad + n:pad + n + pad, :] = jnp.zeros((pad, LANES), F32)

    def one_sequence(bi):
        xpad[pad:pad + n, :] = xc_ref[bi]
        for r0 in range(0, n, rows):
            dst = (r0 // lc) * pitch + r0 % lc
            xconv = bconv
            for j in range(C_CONV):
                xconv = xconv + xpad[pad + r0 + j - 1:pad + r0 + j - 1 + rows, :] * wconv[j:j + 1, :]
            g = _dot(xconv.astype(BF16), wg) + bg
            xh = 0.5 * xconv
            for d, (a_ref, u_ref) in enumerate(((af, uf), (ab, ub))):
                t_r = jnp.tanh(g[:, (2 * d) * LANES:(2 * d + 1) * LANES])
                t_i = jnp.tanh(g[:, (2 * d + 1) * LANES:(2 * d + 2) * LANES])
                kd = k1[d:d + 1, :]
                w = kd * t_r + kd
                a = jnp.exp2(w * (-LOG2E))
                zz = jnp.tanh(w) * (1.0 + a * a)
                mult = zz * lax.rsqrt(jnp.maximum(zz, tiny))
                a_ref[dst:dst + rows, :] = a
                u_ref[dst:dst + rows, :] = mult * (t_i * xh + xh)

        def scan_body(j, carry):
            hf, pf, hb, pb = carry
            fi = pl.ds(j, SUBLANES, stride=pitch)
            ri = pl.ds(lc - 1 - j, SUBLANES, stride=pitch)
            a_f = af[fi, :]
            a_b = ab[ri, :]
            hf = a_f * hf + uf[fi, :]
            hb = a_b * hb + ub[ri, :]
            pf = a_f * pf
            pb = a_b * pb
            uf[fi, :] = hf
            af[fi, :] = pf
            ub[ri, :] = hb
            ab[ri, :] = pb
            return hf, pf, hb, pb

        zeros = jnp.zeros((SUBLANES, LANES), F32)
        ones = jnp.ones((SUBLANES, LANES), F32)
        hf_end, af_end, hb_end, ab_end = lax.fori_loop(0, lc, scan_body, (zeros, ones, zeros, ones), unroll=8)

        h0 = h0_ref[bi]
        cf = [h0[0:1, :]]
        for s in range(SUBLANES):
            cf.append(af_end[s:s + 1, :] * cf[s] + hf_end[s:s + 1, :])
        cb = [None] * (SUBLANES + 1)
        cb[SUBLANES] = h0[1:2, :]
        for s in range(SUBLANES - 1, -1, -1):
            cb[s] = ab_end[s:s + 1, :] * cb[s + 1] + hb_end[s:s + 1, :]
        st_ref[bi] = jnp.concatenate([cf[SUBLANES], cb[0]], axis=0)

        for s in range(SUBLANES):
            src = slice(s * pitch, s * pitch + lc)
            rs = slice(s * lc, (s + 1) * lc)
            hf = uf[src, :] + af[src, :] * cf[s]
            hb = ub[src, :] + ab[src, :] * cb[s + 1]
            y_ref[bi, rs, :] = (gg_ref[bi, rs, :] * (hf + hb)).astype(BF16)

    if n_seq == 1:
        one_sequence(0)
    else:
        def seq_body(bi, carry):
            one_sequence(bi)
            return carry
        lax.fori_loop(0, n_seq, seq_body, 0)


def _lru_call(xc, gc, lw, h0):
    nb, n, _ = xc.shape
    ng = C_WIDTH // LANES
    lc = n // SUBLANES
    rows = min(lc, 512)
    n_scan = SUBLANES * _chunk_pitch(lc)
    bb = _pick_tile(nb, max(1, 1024 // n))
    seq = lambda b, g: (b, 0, g)
    return pl.pallas_call(
        functools.partial(_lru_kernel, seq_len=n, rows=rows, n_seq=bb),
        grid=(nb // bb, ng),
        in_specs=[
            pl.BlockSpec((bb, n, LANES), seq),
            pl.BlockSpec((bb, n, LANES), seq),
            pl.BlockSpec((C_CONV, LANES), lambda b, g: (0, g)),
            pl.BlockSpec((1, LANES), lambda b, g: (0, g)),
            pl.BlockSpec((2, LANES), lambda b, g: (0, g)),
            pl.BlockSpec((1, LANES, 4 * LANES), lambda b, g: (g, 0, 0)),
            pl.BlockSpec((1, 1, 4 * LANES), lambda b, g: (g, 0, 0)),
            pl.BlockSpec((bb, 2, LANES), seq),
        ],
        out_specs=[pl.BlockSpec((bb, n, LANES), seq), pl.BlockSpec((bb, 2, LANES), seq)],
        out_shape=[jax.ShapeDtypeStruct((nb, n, C_WIDTH), BF16),
                   jax.ShapeDtypeStruct((nb, 2, C_WIDTH), F32)],
        scratch_shapes=[pltpu.VMEM((n + 2 * SUBLANES, LANES), F32)] + [pltpu.VMEM((n_scan, LANES), F32)] * 4,
        compiler_params=_cparams("arbitrary", "arbitrary"),
        name="rglru",
    )(xc, gc, lw["w_conv"], lw["b_conv"], lw["lam"], lw["wgate"], lw["bgate"], h0)


def _merge_kernel(x_ref, ya_ref, yb_ref, yc_ref, gl_ref, g1_ref, gpost_ref, wbr_ref, wo_ref, o_ref):
    m = None
    for nbr, y_ref in enumerate((ya_ref, yb_ref, yc_ref)):
        gate = jax.nn.sigmoid(gl_ref[:, nbr * D_MODEL:(nbr + 1) * D_MODEL])
        term = gate * _dot(y_ref[...], wbr_ref[nbr])
        m = term if m is None else m + term
    o = _dot(m.astype(BF16), wo_ref[...])
    o_ref[...] = x_ref[...] + g1_ref[0] * _rms(o, gpost_ref[...])


def _merge_call(x, ya, yb, yc, gl, mod3, mod_row, lw, *, tm):
    t = x.shape[0]
    row = lambda i: (i, 0)
    return pl.pallas_call(
        _merge_kernel,
        grid=(t // tm,),
        in_specs=[
            pl.BlockSpec((tm, D_MODEL), row),
            pl.BlockSpec((tm, MIX_W), row),
            pl.BlockSpec((tm, MIX_W), row),
            pl.BlockSpec((tm, MIX_W), row),
            pl.BlockSpec((tm, N_BRANCH * D_MODEL), row),
            pl.BlockSpec((1, 1, D_MODEL), lambda i: (mod_row(i), 0, 2)),
            pl.BlockSpec((1, D_MODEL), lambda i: (0, 0)),
            pl.BlockSpec((N_BRANCH, MIX_W, D_MODEL), lambda i: (0, 0, 0)),
            pl.BlockSpec((D_MODEL, D_MODEL), lambda i: (0, 0)),
        ],
        out_specs=pl.BlockSpec((tm, D_MODEL), row),
        out_shape=jax.ShapeDtypeStruct((t, D_MODEL), F32),
        compiler_params=_cparams("arbitrary"),
        name="merge",
    )(x, ya, yb, yc, gl, mod3, lw["g_post1"], lw["w_branch"], lw["w_out"])


def _mlp_kernel(x_ref, sh_ref, sc_ref, g2_ref, gpre_ref, gpost_ref, w1_ref, w2_ref, o_ref, *, ff_chunk):
    x = x_ref[...]
    hb = (_rms(x, gpre_ref[...]) * (1.0 + sc_ref[0]) + sh_ref[0]).astype(BF16)
    f = None
    for c0 in range(0, D_FF, ff_chunk):
        u = jnp.maximum(_dot(hb, w1_ref[:, c0:c0 + ff_chunk]), 0.0)
        part = _dot((u * u).astype(BF16), w2_ref[c0:c0 + ff_chunk, :])
        f = part if f is None else f + part
    o_ref[...] = x + g2_ref[0] * _rms(f, gpost_ref[...])


def _mlp_call(x, mod3, mod_row, lw, *, tm):
    t = x.shape[0]
    row = lambda i: (i, 0)
    modspec = lambda col: pl.BlockSpec((1, 1, D_MODEL), lambda i: (mod_row(i), 0, col))
    return pl.pallas_call(
        functools.partial(_mlp_kernel, ff_chunk=1024),
        grid=(t // tm,),
        in_specs=[
            pl.BlockSpec((tm, D_MODEL), row),
            modspec(3), modspec(4), modspec(5),
            pl.BlockSpec((1, D_MODEL), lambda i: (0, 0)),
            pl.BlockSpec((1, D_MODEL), lambda i: (0, 0)),
            pl.BlockSpec((D_MODEL, D_FF), lambda i: (0, 0)),
            pl.BlockSpec((D_FF, D_MODEL), lambda i: (0, 0)),
        ],
        out_specs=pl.BlockSpec((tm, D_MODEL), row),
        out_shape=jax.ShapeDtypeStruct((t, D_MODEL), F32),
        compiler_params=_cparams("arbitrary"),
        name="mlp",
    )(x, mod3, mod3, mod3, lw["g_pre2"], lw["g_post2"], lw["w_ff1"], lw["w_ff2"])


def _rope_tables(n):
    pos = jnp.arange(n)
    rows = (pos // GRID_W).astype(F32)[:, None]
    cols = (pos % GRID_W).astype(F32)[:, None]

    def pattern(dim):
        half = dim // 4
        inv = jnp.power(ROPE_BASE, -jnp.arange(half, dtype=F32) * (2.0 / (dim // 2)))
        ar, ac = rows * inv, cols * inv
        c = jnp.concatenate([jnp.cos(ar), jnp.cos(ar), jnp.cos(ac), jnp.cos(ac)], axis=1)
        s = jnp.concatenate([-jnp.sin(ar), jnp.sin(ar), -jnp.sin(ac), jnp.sin(ac)], axis=1)
        return c, s

    c64, s64 = pattern(A_HEAD_DIM)
    ca = jnp.tile(c64, (1, LANES // A_HEAD_DIM))
    sa = jnp.tile(s64, (1, LANES // A_HEAD_DIM))
    c32, s32 = pattern(B_ROPE)
    tail = HEAD_PAD - B_QK_DIM
    cb = jnp.concatenate([jnp.ones((n, B_NOPE), F32), c32, jnp.ones((n, tail), F32)], axis=1)
    sb = jnp.concatenate([jnp.zeros((n, B_NOPE), F32), s32, jnp.zeros((n, tail), F32)], axis=1)
    return (ca, sa, cb, sb, (ca * A_QSCALE).T, (sa * A_QSCALE).T, (cb * B_QSCALE).T, (sb * B_QSCALE).T)


def _prep_weights(p):
    w_in = p["w_in"]
    qw = A_HEADS * A_HEAD_DIM
    o_k = qw
    o_v = o_k + LANES
    o_cq = o_v + LANES
    o_ckv = o_cq + B_Q_LORA
    o_kr = o_ckv + B_KV_LORA
    o_xc = o_kr + B_ROPE
    o_gl = o_xc + 2 * C_WIDTH
    w_q, w_k, w_v = w_in[:, :, :o_k], w_in[:, :, o_k:o_v], w_in[:, :, o_v:o_cq]
    w_cq, w_ckv, w_kr = w_in[:, :, o_cq:o_ckv], w_in[:, :, o_ckv:o_kr], w_in[:, :, o_kr:o_xc]
    kr_placed = jnp.pad(w_kr, ((0, 0), (0, 0), (B_NOPE, HEAD_PAD - B_QK_DIM)))
    pad_heads = lambda w, dh: jnp.pad(
        w.reshape(DEPTH, w.shape[1], B_HEADS, dh), ((0, 0), (0, 0), (0, 0), (0, HEAD_PAD - dh))
    ).reshape(DEPTH, w.shape[1], B_HEADS * HEAD_PAD)
    tr = lambda w: jnp.swapaxes(w, 1, 2)

    eye = jnp.eye(C_BLOCKS, dtype=F32)
    ng = C_WIDTH // LANES

    def gate_groups(w):
        full = (w[:, :, :, None, :] * eye[None, :, None, :, None]).reshape(DEPTH, C_WIDTH, C_WIDTH)
        full = full.reshape(DEPTH, ng, LANES, ng, LANES)
        return jnp.stack([full[:, g, :, g, :] for g in range(ng)], axis=1)

    w_ra, w_ri, b_ra, b_ri = p["w_ra"], p["w_ri"], p["b_ra"], p["b_ri"]
    wgate = jnp.concatenate([gate_groups(w_ra[:, 0]), gate_groups(w_ri[:, 0]),
                             gate_groups(w_ra[:, 1]), gate_groups(w_ri[:, 1])], axis=3)
    grp = lambda b: b.reshape(DEPTH, ng, 1, LANES)
    bgate = jnp.concatenate([grp(b_ra[:, 0]), grp(b_ri[:, 0]), grp(b_ra[:, 1]), grp(b_ri[:, 1])], axis=3)

    row = lambda g: g[:, None, :]
    return dict(
        g_pre1=row(p["g_pre1"]), g_post1=row(p["g_post1"]), g_pre2=row(p["g_pre2"]), g_post2=row(p["g_post2"]),
        g_q=row(p["g_q"]), g_kv=row(p["g_kv"]),
        wqat=tr(w_q).astype(BF16),
        wkv=jnp.concatenate([w_k, w_v], axis=2).astype(BF16),
        wvat=tr(w_v).astype(BF16),
        wb=jnp.concatenate([w_cq, kr_placed, w_ckv], axis=2).astype(BF16),
        wl=w_in[:, :, o_xc:o_gl].astype(BF16), wg=w_in[:, :, o_gl:].astype(BF16),
        wqt=tr(pad_heads(p["w_q_up"], B_QK_DIM)).astype(BF16),
        wuk=pad_heads(p["w_uk"], B_NOPE).astype(BF16),
        wuvt=tr(p["w_uv"]).astype(BF16),
        w_conv=p["w_conv"], b_conv=row(p["b_conv"]), lam=p["lam"],
        wgate=(0.5 * wgate).astype(BF16), bgate=0.5 * bgate,
        w_branch=p["w_branch"].astype(BF16), w_out=p["w_out"].astype(BF16),
        w_ff1=p["w_ff1"].astype(BF16), w_ff2=p["w_ff2"].astype(BF16),
    )


def _pick_tile(n, target, multiple=1):
    t = max(multiple, min(n, target) // multiple * multiple)
    while n % t:
        t -= multiple
    return t


def kernel(x_prompt, x_sample, cache_attn_k, cache_attn_v, cache_mla_ckv, cache_mla_krope, state_lru, c, c_ctx, w_ada, b_ada, g_pre1, g_post1, g_pre2, g_post2, w_in, sink, g_q, w_q_up, g_kv, w_uk, w_uv, w_conv, b_conv, lam, w_ra, b_ra, w_ri, b_ri, w_branch, w_out, w_ff1, w_ff2):
    nbc, seq, _ = x_prompt.shape
    nbl, n_lat, _ = x_sample.shape
    past = cache_attn_k.shape[2]
    assert n_lat % GRID_W == 0 and n_lat >= 3 * Q_BLOCK and n_lat % Q_BLOCK == 0
    assert seq % (SUBLANES * SUBLANES) == 0 and n_lat % (SUBLANES * SUBLANES) == 0

    weights = _prep_weights(dict(
        g_pre1=g_pre1, g_post1=g_post1, g_pre2=g_pre2, g_post2=g_post2, w_in=w_in, g_q=g_q,
        w_q_up=w_q_up, g_kv=g_kv, w_uk=w_uk, w_uv=w_uv, w_conv=w_conv, b_conv=b_conv, lam=lam,
        w_ra=w_ra, b_ra=b_ra, w_ri=w_ri, b_ri=b_ri, w_branch=w_branch, w_out=w_out,
        w_ff1=w_ff1, w_ff2=w_ff2))
    tables = _rope_tables(n_lat)

    n_mod = -(-(nbl + 1) // SUBLANES) * SUBLANES
    cmat = jnp.zeros((n_mod, D_MODEL), F32).at[:nbl].set(c).at[nbl].set(c_ctx)
    mod3 = _ada_call(cmat, w_ada, b_ada).reshape(DEPTH * n_mod, 1, N_ADA * D_MODEL)

    place = jnp.pad(jnp.eye(B_ROPE, dtype=BF16), ((0, 0), (B_NOPE, HEAD_PAD - B_QK_DIM)))
    kc_mla, vtc_mla = _mla_cache_call(cache_mla_ckv, cache_mla_krope, weights["wuk"], weights["wuvt"], place)
    kc_a = jnp.swapaxes(cache_attn_k, 2, 3).astype(BF16)
    ones_rows = jnp.zeros((nbl, DEPTH, A_KV_HEADS, V_ROWS - A_HEAD_DIM, past), BF16).at[:, :, :, 0].set(1)
    vtc_a = jnp.concatenate(
        [jnp.transpose(cache_attn_v, (0, 1, 3, 4, 2)).astype(BF16), ones_rows], axis=3)

    tm_p = _pick_tile(seq, 256)
    tm_s = _pick_tile(n_lat, 256)
    y_p = x_prompt.reshape(nbc * seq, D_MODEL)
    y_s = x_sample.reshape(nbl * n_lat, D_MODEL)
    zero_state = jnp.zeros((nbc, 2, C_WIDTH), F32)
    ks_l, vs_l, ckv_l, kr_l, st_l = [], [], [], [], []
    for l in range(DEPTH):
        lw = {k: v[l] for k, v in weights.items()}
        row_p = lambda i, l=l: l * n_mod + nbl
        row_s = lambda i, l=l: l * n_mod + (i * tm_s) // n_lat

        (qat, ka, vat, qbt, kb, vbt, xc, gc, gl, kaf, vaf, ckvn, krf) = _inproj_call(
            y_p, mod3, row_p, lw, None, latent=False, seq_len=seq, tm=tm_p)
        r3 = lambda a: a.reshape(nbc, seq, a.shape[-1])
        ya = _gqa_call(sink[l], qat, ka, vat)
        yb = _mla_call(qbt, r3(kb), vbt, tq=seq, tk=seq)
        yc, st = _lru_call(r3(xc), r3(gc), lw, zero_state)
        f2 = lambda a: a.reshape(nbc * seq, a.shape[-1])
        y_p = _merge_call(y_p, f2(ya), f2(yb), f2(yc), gl, mod3, row_p, lw, tm=tm_p)
        y_p = _mlp_call(y_p, mod3, row_p, lw, tm=tm_p)
        ks_l.append(kaf.reshape(nbc, seq, A_KV_HEADS, A_HEAD_DIM))
        vs_l.append(vaf.reshape(nbc, seq, A_KV_HEADS, A_HEAD_DIM))
        ckv_l.append(ckvn.reshape(nbc, seq, B_KV_LORA))
        kr_l.append(krf.reshape(nbc, seq, B_ROPE))
        st_l.append(st)

        (qat, ka, vat, qbt, kb, vbt, xc, gc, gl) = _inproj_call(
            y_s, mod3, row_s, lw, tables, latent=True, seq_len=n_lat, tm=tm_s)
        r3 = lambda a: a.reshape(nbl, n_lat, a.shape[-1])
        ya = _gqa_call(sink[l], qat, ka, vat, kc_a, vtc_a, layer=l)
        yb = _mla_call(qbt, r3(kb), vbt, kc_mla, vtc_mla, layer=l, tq=_pick_tile(n_lat, 256, LANES),
                       tk=_pick_tile(math.gcd(n_lat, past), 512, LANES))
        yc, _ = _lru_call(r3(xc), r3(gc), lw, state_lru[:, l])
        f2 = lambda a: a.reshape(nbl * n_lat, a.shape[-1])
        y_s = _merge_call(y_s, f2(ya), f2(yb), f2(yc), gl, mod3, row_s, lw, tm=tm_s)
        y_s = _mlp_call(y_s, mod3, row_s, lw, tm=tm_s)

    return (y_p.reshape(nbc, seq, D_MODEL), y_s.reshape(nbl, n_lat, D_MODEL),
            jnp.stack(ks_l, axis=1), jnp.stack(vs_l, axis=1), jnp.stack(ckv_l, axis=1),
            jnp.stack(kr_l, axis=1), jnp.stack(st_l, axis=1))
```

```python
import functools
import math

import jax
import jax.numpy as jnp
from jax import lax
from jax.experimental import pallas as pl
from jax.experimental.pallas import tpu as pltpu

F32 = jnp.float32
BF16 = jnp.bfloat16

D_MODEL = 1024
DEPTH = 4
GRID_W = 64
Q_BLOCK = 128
A_HEADS = 8
A_KV_HEADS = 2
A_HEAD_DIM = 64
A_GROUP = A_HEADS // A_KV_HEADS
A_WINDOW = 128
B_HEADS = 8
B_Q_LORA = 384
B_KV_LORA = 256
B_NOPE = 64
B_ROPE = 32
B_QK_DIM = B_NOPE + B_ROPE
B_V_DIM = 64
C_WIDTH = 512
C_BLOCKS = 8
C_BLOCK_W = C_WIDTH // C_BLOCKS
C_CONV = 4
C_RG = 8.0
MIX_W = 512
N_BRANCH = 3
D_FF = 4 * D_MODEL
N_ADA = 6
ROPE_BASE = 10000.0
EPS = 1e-6
NEG_INF = -1e30
LOG2E = math.log2(math.e)

LANES = 128
SUBLANES = 8
HEAD_PAD = 128
GQA_Q_TILE = 128
V_ROWS = 80
VMEM_LIMIT = 56 * 1024 * 1024
A_QSCALE = A_HEAD_DIM ** -0.5 * LOG2E
B_QSCALE = B_QK_DIM ** -0.5 * LOG2E


def _cparams(*sem):
    return pltpu.CompilerParams(dimension_semantics=sem, vmem_limit_bytes=VMEM_LIMIT)


def _dot(a, b):
    return jnp.dot(a, b, preferred_element_type=F32)


def _dot_nt(a, b):
    return lax.dot_general(a, b, (((1,), (1,)), ((), ())), preferred_element_type=F32)


def _rms(x, g):
    return x * lax.rsqrt(jnp.mean(x * x, axis=-1, keepdims=True) + EPS) * g


def _rope(x, c, s, half, axis):
    pos = lax.broadcasted_iota(jnp.int32, x.shape, axis)
    lo = (pos & (2 * half - 1)) < half
    partner = jnp.where(lo, pltpu.roll(x, LANES - half, axis), pltpu.roll(x, half, axis))
    return x * c + partner * s


def _ada_kernel(c_ref, w_ref, b_ref, o_ref):
    c = c_ref[...]
    a = (c * jax.nn.sigmoid(c)).astype(BF16)
    o_ref[0] = _dot(a, w_ref[0].astype(BF16)) + b_ref[0]


def _ada_call(cmat, w_ada, b_ada):
    nb = cmat.shape[0]
    n_out = N_ADA * D_MODEL
    tn = 1536
    return pl.pallas_call(
        _ada_kernel,
        grid=(DEPTH, n_out // tn),
        in_specs=[
            pl.BlockSpec((nb, D_MODEL), lambda l, j: (0, 0)),
            pl.BlockSpec((1, D_MODEL, tn), lambda l, j: (l, 0, j)),
            pl.BlockSpec((1, 1, tn), lambda l, j: (l, 0, j)),
        ],
        out_specs=pl.BlockSpec((1, nb, tn), lambda l, j: (l, 0, j)),
        out_shape=jax.ShapeDtypeStruct((DEPTH, nb, n_out), F32),
        compiler_params=_cparams("arbitrary", "arbitrary"),
        name="ada_mod",
    )(cmat, w_ada, b_ada.reshape(DEPTH, 1, n_out))


def _ones_row_block(cols):
    r = lax.broadcasted_iota(jnp.int32, (V_ROWS - B_V_DIM, cols), 0)
    return jnp.where(r == 0, 1.0, 0.0).astype(BF16)


def _inproj_kernel(*refs, latent):
    (x_ref, sh_ref, sc_ref, gpre_ref, wqat_ref, wkv_ref, wvat_ref, wb_ref, wl_ref, wg_ref,
     gq_ref, wqt_ref, gkv_ref, wuk_ref, wuvt_ref) = refs[:15]
    refs = refs[15:]
    if latent:
        ca_ref, sa_ref, cb_ref, sb_ref, caqt_ref, saqt_ref, cbqt_ref, sbqt_ref = refs[:8]
        refs = refs[8:]
        qat_ref, ka_ref, vat_ref, qbt_ref, kb_ref, vbt_ref, xc_ref, gc_ref, gl_ref = refs
    else:
        (qat_ref, ka_ref, vat_ref, qbt_ref, kb_ref, vbt_ref, xc_ref, gc_ref, gl_ref,
         kaf_ref, vaf_ref, ckvn_ref, krf_ref) = refs

    x = x_ref[...]
    h = _rms(x, gpre_ref[...]) * (1.0 + sc_ref[0]) + sh_ref[0]
    hb = h.astype(BF16)

    tm = x.shape[0]
    ones_blk = _ones_row_block(tm)
    qat = _dot_nt(wqat_ref[...], hb)
    pkv = _dot(hb, wkv_ref[...])
    ka = pkv[:, :LANES]
    vat = _dot_nt(wvat_ref[...], hb)
    if latent:
        caqt, saqt = caqt_ref[...], saqt_ref[...]
        qat = jnp.concatenate(
            [_rope(qat[j * LANES:(j + 1) * LANES, :], caqt, saqt, 16, 0)
             for j in range(A_HEADS * A_HEAD_DIM // LANES)], axis=0)
        ka = _rope(ka, ca_ref[...], sa_ref[...], 16, 1)
    else:
        qat = qat * A_QSCALE
        kaf_ref[...] = ka
        vaf_ref[...] = pkv[:, LANES:]
    qat_ref[0] = qat.astype(BF16)
    for g in range(A_KV_HEADS):
        ka_ref[0, g] = ka[:, g * A_HEAD_DIM:(g + 1) * A_HEAD_DIM].astype(BF16)
        vat_ref[0, g, 0:A_HEAD_DIM, :] = vat[g * A_HEAD_DIM:(g + 1) * A_HEAD_DIM, :].astype(BF16)
        vat_ref[0, g, A_HEAD_DIM:V_ROWS, :] = ones_blk

    pb = _dot(hb, wb_ref[...])
    cq = pb[:, :B_Q_LORA]
    krp = pb[:, B_Q_LORA:B_Q_LORA + HEAD_PAD]
    ckv = pb[:, B_Q_LORA + HEAD_PAD:]
    qt = _dot_nt(wqt_ref[...], _rms(cq, gq_ref[...]).astype(BF16))
    ckvn = _rms(ckv, gkv_ref[...])
    ckvb = ckvn.astype(BF16)
    kn = _dot(ckvb, wuk_ref[...])
    vt = _dot_nt(wuvt_ref[...], ckvb)
    for hd in range(B_HEADS):
        vbt_ref[0, hd, 0:B_V_DIM, :] = vt[hd * B_V_DIM:(hd + 1) * B_V_DIM, :].astype(BF16)
        vbt_ref[0, hd, B_V_DIM:V_ROWS, :] = ones_blk
    if latent:
        cbqt, sbqt = cbqt_ref[...], sbqt_ref[...]
        qt = jnp.concatenate(
            [_rope(qt[j * HEAD_PAD:(j + 1) * HEAD_PAD, :], cbqt, sbqt, 8, 0) for j in range(B_HEADS)], axis=0)
        krp = _rope(krp, cb_ref[...], sb_ref[...], 8, 1)
    else:
        qt = qt * B_QSCALE
        ckvn_ref[...] = ckvn
        krf_ref[...] = krp[:, B_NOPE:B_QK_DIM]
    qbt_ref[0] = qt.astype(BF16)
    kb_ref[...] = jnp.concatenate(
        [kn[:, j * HEAD_PAD:(j + 1) * HEAD_PAD] + krp for j in range(B_HEADS)], axis=1).astype(BF16)

    pc = _dot(hb, wl_ref[...])
    xc_ref[...] = pc[:, :C_WIDTH]
    gc_ref[...] = jax.nn.gelu(pc[:, C_WIDTH:])
    gl_ref[...] = _dot(hb, wg_ref[...])


def _inproj_call(x, mod3, mod_row, lw, tables, *, latent, seq_len, tm):
    t = x.shape[0]
    nb = t // seq_len
    tps = seq_len // tm
    row = lambda i: (i, 0)
    const = lambda i: (0, 0)
    modspec = lambda col: pl.BlockSpec((1, 1, D_MODEL), lambda i: (mod_row(i), 0, col))
    wspec = lambda w: pl.BlockSpec(w.shape, const, pipeline_mode=pl.Buffered(1))
    sfx = "_lat" if latent else "_ctx"
    weights = [lw["g_pre1"], lw["wqat"], lw["wkv"], lw["wvat"], lw["wb"], lw["wl"], lw["wg"],
               lw["g_q"], lw["wqt"], lw["g_kv"], lw["wuk"], lw["wuvt"]]
    in_specs = [pl.BlockSpec((tm, D_MODEL), row), modspec(0), modspec(1)] + [wspec(w) for w in weights]
    args = [x, mod3, mod3] + weights
    if latent:
        rspec = pl.BlockSpec((tm, LANES), lambda i: (i % tps, 0))
        cspec = pl.BlockSpec((LANES, tm), lambda i: (0, i % tps))
        in_specs += [rspec] * 4 + [cspec] * 4
        args += list(tables)
    rows2 = lambda w, dt: (jax.ShapeDtypeStruct((t, w), dt), pl.BlockSpec((tm, w), row))
    kw = B_HEADS * HEAD_PAD
    qw = A_HEADS * A_HEAD_DIM
    outs = [
        (jax.ShapeDtypeStruct((nb, qw, seq_len), BF16),
         pl.BlockSpec((1, qw, tm), lambda i: (i // tps, 0, i % tps))),
        (jax.ShapeDtypeStruct((nb, A_KV_HEADS, seq_len, A_HEAD_DIM), BF16),
         pl.BlockSpec((1, A_KV_HEADS, tm, A_HEAD_DIM), lambda i: (i // tps, 0, i % tps, 0))),
        (jax.ShapeDtypeStruct((nb, A_KV_HEADS, V_ROWS, seq_len), BF16),
         pl.BlockSpec((1, A_KV_HEADS, V_ROWS, tm), lambda i: (i // tps, 0, 0, i % tps))),
        (jax.ShapeDtypeStruct((nb, kw, seq_len), BF16),
         pl.BlockSpec((1, kw, tm), lambda i: (i // tps, 0, i % tps))),
        rows2(kw, BF16),
        (jax.ShapeDtypeStruct((nb, B_HEADS, V_ROWS, seq_len), BF16),
         pl.BlockSpec((1, B_HEADS, V_ROWS, tm), lambda i: (i // tps, 0, 0, i % tps))),
        rows2(C_WIDTH, F32), rows2(C_WIDTH, F32), rows2(N_BRANCH * D_MODEL, F32),
    ]
    if not latent:
        outs += [rows2(LANES, F32), rows2(LANES, F32), rows2(B_KV_LORA, F32), rows2(B_ROPE, F32)]
    return pl.pallas_call(
        functools.partial(_inproj_kernel, latent=latent),
        grid=(t // tm,),
        in_specs=in_specs,
        out_specs=[o[1] for o in outs],
        out_shape=[o[0] for o in outs],
        compiler_params=_cparams("arbitrary"),
        name="inproj" + sfx,
    )(*args)


def _mla_cache_kernel(ckv_ref, kr_ref, wuk_ref, wuvt_ref, place_ref, k_ref, vt_ref):
    ckv = ckv_ref[0, 0].astype(BF16)
    kn = _dot(ckv, wuk_ref[0])
    krp = _dot(kr_ref[0, 0].astype(BF16), place_ref[...])
    k_ref[0, 0] = jnp.concatenate(
        [kn[:, j * HEAD_PAD:(j + 1) * HEAD_PAD] + krp for j in range(B_HEADS)], axis=1).astype(BF16)
    vt = _dot_nt(wuvt_ref[0], ckv)
    ones_blk = _ones_row_block(ckv.shape[0])
    for hd in range(B_HEADS):
        vt_ref[0, 0, hd, 0:B_V_DIM, :] = vt[hd * B_V_DIM:(hd + 1) * B_V_DIM, :].astype(BF16)
        vt_ref[0, 0, hd, B_V_DIM:V_ROWS, :] = ones_blk


def _mla_cache_call(cache_ckv, cache_krope, wuk, wuvt, place):
    nb, _, p, _ = cache_ckv.shape
    kw, vw = B_HEADS * HEAD_PAD, B_HEADS * B_V_DIM
    return pl.pallas_call(
        _mla_cache_kernel,
        grid=(DEPTH, nb),
        in_specs=[
            pl.BlockSpec((1, 1, p, B_KV_LORA), lambda l, b: (b, l, 0, 0)),
            pl.BlockSpec((1, 1, p, B_ROPE), lambda l, b: (b, l, 0, 0)),
            pl.BlockSpec((1, B_KV_LORA, kw), lambda l, b: (l, 0, 0)),
            pl.BlockSpec((1, vw, B_KV_LORA), lambda l, b: (l, 0, 0)),
            pl.BlockSpec((B_ROPE, HEAD_PAD), lambda l, b: (0, 0)),
        ],
        out_specs=[
            pl.BlockSpec((1, 1, p, kw), lambda l, b: (b, l, 0, 0)),
            pl.BlockSpec((1, 1, B_HEADS, V_ROWS, p), lambda l, b: (b, l, 0, 0, 0)),
        ],
        out_shape=[jax.ShapeDtypeStruct((nb, DEPTH, p, kw), BF16),
                   jax.ShapeDtypeStruct((nb, DEPTH, B_HEADS, V_ROWS, p), BF16)],
        compiler_params=_cparams("arbitrary", "arbitrary"),
        name="mla_cache_kv",
    )(cache_ckv, cache_krope, wuk, wuvt, place)


def _gqa_kernel(*refs, latent, seq_len):
    if latent:
        sink_ref, qt_ref, k_ref, vt_ref, kc_ref, vtc_ref, o_ref = refs
    else:
        sink_ref, qt_ref, k_ref, vt_ref, o_ref = refs
    nq = qt_ref.shape[2]
    cols = A_GROUP * nq
    if latent:
        win = nq + 2 * A_WINDOW
        i = pl.program_id(1)
        start = pl.multiple_of(jnp.clip(i * nq - A_WINDOW, 0, seq_len - win), A_WINDOW)
        keys = pl.ds(start, win)
        kpos = start + lax.broadcasted_iota(jnp.int32, (win, cols), 0)
        qpos = i * nq + (lax.broadcasted_iota(jnp.int32, (win, cols), 1) & (nq - 1))
        valid = jnp.abs(qpos - kpos) <= A_WINDOW
    else:
        keys = slice(None)
    scores = []
    for g in range(A_KV_HEADS):
        qgt = jnp.concatenate(
            [qt_ref[0, hd * A_HEAD_DIM:(hd + 1) * A_HEAD_DIM, :]
             for hd in range(g * A_GROUP, (g + 1) * A_GROUP)], axis=1)
        s_l = _dot(k_ref[0, g, keys, :], qgt)
        s_c = _dot(kc_ref[0, 0, g], qgt) if latent else None
        scores.append((s_l, s_c))
    heads_t = []
    for g in range(A_KV_HEADS):
        s_l, s_c = scores[g]
        sink2 = jnp.concatenate(
            [jnp.full((1, nq), sink_ref[hd] * LOG2E, F32)
             for hd in range(g * A_GROUP, (g + 1) * A_GROUP)], axis=1)
        m = sink2
        if latent:
            s_l = jnp.where(valid, s_l, NEG_INF)
            m = jnp.maximum(m, jnp.max(s_c, axis=0, keepdims=True))
        m = jnp.maximum(m, jnp.max(s_l, axis=0, keepdims=True))
        acc = _dot(vt_ref[0, g, :, keys], jnp.exp2(s_l - m).astype(BF16))
        if latent:
            acc = acc + _dot(vtc_ref[0, 0, g], jnp.exp2(s_c - m).astype(BF16))
        den = acc[A_HEAD_DIM:A_HEAD_DIM + 1, :] + jnp.exp2(sink2 - m)
        o_t = acc[:A_HEAD_DIM, :] / den
        heads_t += [o_t[:, j * nq:(j + 1) * nq] for j in range(A_GROUP)]
    o_ref[0] = jnp.concatenate(heads_t, axis=0).T.astype(BF16)


def _gqa_call(sink_l, qt, k, vt, kc=None, vtc=None, layer=0):
    nb, qw, n = qt.shape
    latent = kc is not None
    smem = pl.BlockSpec(memory_space=pltpu.SMEM)
    nq = GQA_Q_TILE if latent else n
    in_specs = [smem,
                pl.BlockSpec((1, qw, nq), lambda b, i: (b, 0, i)),
                pl.BlockSpec((1, A_KV_HEADS, n, A_HEAD_DIM), lambda b, i: (b, 0, 0, 0)),
                pl.BlockSpec((1, A_KV_HEADS, V_ROWS, n), lambda b, i: (b, 0, 0, 0))]
    args = [sink_l, qt, k, vt]
    if latent:
        p = kc.shape[3]
        in_specs += [pl.BlockSpec((1, 1, A_KV_HEADS, p, A_HEAD_DIM), lambda b, i: (b, layer, 0, 0, 0)),
                     pl.BlockSpec((1, 1, A_KV_HEADS, V_ROWS, p), lambda b, i: (b, layer, 0, 0, 0))]
        args += [kc, vtc]
    return pl.pallas_call(
        functools.partial(_gqa_kernel, latent=latent, seq_len=n),
        grid=(nb, n // nq),
        in_specs=in_specs,
        out_specs=pl.BlockSpec((1, nq, qw), lambda b, i: (b, i, 0)),
        out_shape=jax.ShapeDtypeStruct((nb, n, qw), BF16),
        compiler_params=_cparams("arbitrary", "arbitrary"),
        name="gqa_lat" if latent else "gqa_ctx",
    )(*args)


def _mla_kernel(*refs, has_ctx, n_own, n_ctx, tk):
    if has_ctx:
        qt_ref, k_ref, vt_ref, kc_ref, vtc_ref, o_ref, m_sc, acc_sc, s0, s1, p0, p1, a0, a1 = refs
    else:
        qt_ref, k_ref, vt_ref, o_ref, m_sc, acc_sc, s0, s1, p0, p1, a0, a1 = refs
    s_buf, p_buf, a_buf = (s0, s1), (p0, p1), (a0, a1)
    m_sc[...] = jnp.full(m_sc.shape, -jnp.inf, F32)
    acc_sc[...] = jnp.zeros(acc_sc.shape, F32)

    def span(c):
        return pl.ds(c * tk, tk) if isinstance(c, int) else pl.ds(pl.multiple_of(c * tk, tk), tk)

    def scores(src, c, hd):
        cols = slice(hd * HEAD_PAD, (hd + 1) * HEAD_PAD)
        k = k_ref[0, span(c), cols] if src == "own" else kc_ref[0, 0, span(c), cols]
        s_buf[hd % 2][...] = _dot(k, qt_ref[0, cols, :])

    def softmax(hd):
        s = s_buf[hd % 2][...]
        m_old = m_sc[hd]
        m_new = jnp.maximum(m_old, jnp.max(s, axis=0, keepdims=True))
        p_buf[hd % 2][...] = jnp.exp2(s - m_new).astype(BF16)
        a_buf[hd % 2][...] = jnp.exp2(m_old - m_new)
        m_sc[hd] = m_new

    def weighted_values(src, c, hd):
        vt = vt_ref[0, hd, :, span(c)] if src == "own" else vtc_ref[0, 0, hd, :, span(c)]
        acc_sc[hd] = a_buf[hd % 2][...] * acc_sc[hd] + _dot(vt, p_buf[hd % 2][...])

    def chunk_steps(cur, nxt):
        for hd in range(B_HEADS):
            if hd + 2 < B_HEADS:
                scores(*cur, hd + 2)
            elif nxt is not None:
                scores(*nxt, hd + 2 - B_HEADS)
            if hd + 1 < B_HEADS:
                softmax(hd + 1)
            elif nxt is not None:
                softmax(0)
            weighted_values(*cur, hd)

    def run(src, count, after):
        def body(c, carry):
            chunk_steps((src, c), (src, c + 1))
            return carry
        lax.fori_loop(0, count - 1, body, 0)
        chunk_steps((src, count - 1), after)

    scores("own", 0, 0)
    scores("own", 0, 1)
    softmax(0)
    if has_ctx:
        run("own", n_own, ("ctx", 0))
        run("ctx", n_ctx, None)
    else:
        run("own", n_own, None)
    outs = []
    for hd in range(B_HEADS):
        acc = acc_sc[hd]
        outs.append(acc[:B_V_DIM, :] / acc[B_V_DIM:B_V_DIM + 1, :])
    o_ref[0] = jnp.concatenate(outs, axis=0).T.astype(BF16)


def _mla_call(qt, k, vt, kc=None, vtc=None, layer=0, *, tq, tk):
    nb, n, kw = k.shape
    vw = B_HEADS * B_V_DIM
    has_ctx = kc is not None
    in_specs = [pl.BlockSpec((1, kw, tq), lambda b, i: (b, 0, i)),
                pl.BlockSpec((1, n, kw), lambda b, i: (b, 0, 0)),
                pl.BlockSpec((1, B_HEADS, V_ROWS, n), lambda b, i: (b, 0, 0, 0))]
    args = [qt, k, vt]
    n_ctx = 0
    if has_ctx:
        p = kc.shape[2]
        n_ctx = p // tk
        in_specs += [pl.BlockSpec((1, 1, p, kw), lambda b, i: (b, layer, 0, 0)),
                     pl.BlockSpec((1, 1, B_HEADS, V_ROWS, p), lambda b, i: (b, layer, 0, 0, 0))]
        args += [kc, vtc]
    return pl.pallas_call(
        functools.partial(_mla_kernel, has_ctx=has_ctx, n_own=n // tk, n_ctx=n_ctx, tk=tk),
        grid=(nb, n // tq),
        in_specs=in_specs,
        out_specs=pl.BlockSpec((1, tq, vw), lambda b, i: (b, i, 0)),
        out_shape=jax.ShapeDtypeStruct((nb, n, vw), BF16),
        scratch_shapes=[pltpu.VMEM((B_HEADS, 1, tq), F32), pltpu.VMEM((B_HEADS, V_ROWS, tq), F32),
                        pltpu.VMEM((tk, tq), F32), pltpu.VMEM((tk, tq), F32),
                        pltpu.VMEM((tk, tq), BF16), pltpu.VMEM((tk, tq), BF16),
                        pltpu.VMEM((1, tq), F32), pltpu.VMEM((1, tq), F32)],
        compiler_params=_cparams("arbitrary", "arbitrary"),
        name="mla_lat" if has_ctx else "mla_ctx",
    )(*args)


def _chunk_pitch(lc):
    return lc if (lc // SUBLANES) % 2 == 1 else lc + SUBLANES


def _lru_kernel(xc_ref, gg_ref, wconv_ref, bconv_ref, lam_ref, wg_ref, bg_ref, h0_ref,
                y_ref, st_ref, xpad, af, uf, ab, ub, *, seq_len, rows, n_seq):
    n = seq_len
    lc = n // SUBLANES
    pitch = _chunk_pitch(lc)
    pad = SUBLANES
    wconv = wconv_ref[...]
    bconv = bconv_ref[...]
    z = -lam_ref[...]
    softplus = jnp.maximum(z, 0.0) + jnp.log1p(jnp.exp(-jnp.abs(z)))
    k1 = (0.5 * C_RG) * softplus
    wg = wg_ref[0]
    bg = bg_ref[0]
    tiny = float(jnp.finfo(F32).tiny)
    xpad[0:pad, :] = jnp.zeros((pad, LANES), F32)
    xpad[pad + n:pad + n + pad, :] = jnp.zeros((pad, LANES), F32)

    def one_sequence(bi):
        xpad[pad:pad + n, :] = xc_ref[bi]
        for r0 in range(0, n, rows):
            dst = (r0 // lc) * pitch + r0 % lc
            xconv = bconv
            for j in range(C_CONV):
                xconv = xconv + xpad[pad + r0 + j - 1:pad + r0 + j - 1 + rows, :] * wconv[j:j + 1, :]
            g = _dot(xconv.astype(BF16), wg) + bg
            xh = 0.5 * xconv
            for d, (a_ref, u_ref) in enumerate(((af, uf), (ab, ub))):
                t_r = jnp.tanh(g[:, (2 * d) * LANES:(2 * d + 1) * LANES])
                t_i = jnp.tanh(g[:, (2 * d + 1) * LANES:(2 * d + 2) * LANES])
                kd = k1[d:d + 1, :]
                w = kd * t_r + kd
                a = jnp.exp2(w * (-LOG2E))
                zz = jnp.tanh(w) * (1.0 + a * a)
                mult = zz * lax.rsqrt(jnp.maximum(zz, tiny))
                a_ref[dst:dst + rows, :] = a
                u_ref[dst:dst + rows, :] = mult * (t_i * xh + xh)

        def scan_body(j, carry):
            hf, pf, hb, pb = carry
            fi = pl.ds(j, SUBLANES, stride=pitch)
            ri = pl.ds(lc - 1 - j, SUBLANES, stride=pitch)
            a_f = af[fi, :]
            a_b = ab[ri, :]
            hf = a_f * hf + uf[fi, :]
            hb = a_b * hb + ub[ri, :]
            pf = a_f * pf
            pb = a_b * pb
            uf[fi, :] = hf
            af[fi, :] = pf
            ub[ri, :] = hb
            ab[ri, :] = pb
            return hf, pf, hb, pb

        zeros = jnp.zeros((SUBLANES, LANES), F32)
        ones = jnp.ones((SUBLANES, LANES), F32)
        hf_end, af_end, hb_end, ab_end = lax.fori_loop(0, lc, scan_body, (zeros, ones, zeros, ones), unroll=8)

        h0 = h0_ref[bi]
        cf = [h0[0:1, :]]
        for s in range(SUBLANES):
            cf.append(af_end[s:s + 1, :] * cf[s] + hf_end[s:s + 1, :])
        cb = [None] * (SUBLANES + 1)
        cb[SUBLANES] = h0[1:2, :]
        for s in range(SUBLANES - 1, -1, -1):
            cb[s] = ab_end[s:s + 1, :] * cb[s + 1] + hb_end[s:s + 1, :]
        st_ref[bi] = jnp.concatenate([cf[SUBLANES], cb[0]], axis=0)

        for s in range(SUBLANES):
            src = slice(s * pitch, s * pitch + lc)
            rs = slice(s * lc, (s + 1) * lc)
            hf = uf[src, :] + af[src, :] * cf[s]
            hb = ub[src, :] + ab[src, :] * cb[s + 1]
            y_ref[bi, rs, :] = (gg_ref[bi, rs, :] * (hf + hb)).astype(BF16)

    if n_seq == 1:
        one_sequence(0)
    else:
        def seq_body(bi, carry):
            one_sequence(bi)
            return carry
        lax.fori_loop(0, n_seq, seq_body, 0)


def _lru_call(xc, gc, lw, h0):
    nb, n, _ = xc.shape
    ng = C_WIDTH // LANES
    lc = n // SUBLANES
    rows = min(lc, 512)
    n_scan = SUBLANES * _chunk_pitch(lc)
    bb = _pick_tile(nb, max(1, 1024 // n))
    seq = lambda b, g: (b, 0, g)
    return pl.pallas_call(
        functools.partial(_lru_kernel, seq_len=n, rows=rows, n_seq=bb),
        grid=(nb // bb, ng),
        in_specs=[
            pl.BlockSpec((bb, n, LANES), seq),
            pl.BlockSpec((bb, n, LANES), seq),
            pl.BlockSpec((C_CONV, LANES), lambda b, g: (0, g)),
            pl.BlockSpec((1, LANES), lambda b, g: (0, g)),
            pl.BlockSpec((2, LANES), lambda b, g: (0, g)),
            pl.BlockSpec((1, LANES, 4 * LANES), lambda b, g: (g, 0, 0)),
            pl.BlockSpec((1, 1, 4 * LANES), lambda b, g: (g, 0, 0)),
            pl.BlockSpec((bb, 2, LANES), seq),
        ],
        out_specs=[pl.BlockSpec((bb, n, LANES), seq), pl.BlockSpec((bb, 2, LANES), seq)],
        out_shape=[jax.ShapeDtypeStruct((nb, n, C_WIDTH), BF16),
                   jax.ShapeDtypeStruct((nb, 2, C_WIDTH), F32)],
        scratch_shapes=[pltpu.VMEM((n + 2 * SUBLANES, LANES), F32)] + [pltpu.VMEM((n_scan, LANES), F32)] * 4,
        compiler_params=_cparams("arbitrary", "arbitrary"),
        name="rglru",
    )(xc, gc, lw["w_conv"], lw["b_conv"], lw["lam"], lw["wgate"], lw["bgate"], h0)


def _merge_kernel(x_ref, ya_ref, yb_ref, yc_ref, gl_ref, g1_ref, gpost_ref, wbr_ref, wo_ref, o_ref):
    m = None
    for nbr, y_ref in enumerate((ya_ref, yb_ref, yc_ref)):
        gate = jax.nn.sigmoid(gl_ref[:, nbr * D_MODEL:(nbr + 1) * D_MODEL])
        term = gate * _dot(y_ref[...], wbr_ref[nbr])
        m = term if m is None else m + term
    o = _dot(m.astype(BF16), wo_ref[...])
    o_ref[...] = x_ref[...] + g1_ref[0] * _rms(o, gpost_ref[...])


def _merge_call(x, ya, yb, yc, gl, mod3, mod_row, lw, *, tm):
    t = x.shape[0]
    row = lambda i: (i, 0)
    return pl.pallas_call(
        _merge_kernel,
        grid=(t // tm,),
        in_specs=[
            pl.BlockSpec((tm, D_MODEL), row),
            pl.BlockSpec((tm, MIX_W), row),
            pl.BlockSpec((tm, MIX_W), row),
            pl.BlockSpec((tm, MIX_W), row),
            pl.BlockSpec((tm, N_BRANCH * D_MODEL), row),
            pl.BlockSpec((1, 1, D_MODEL), lambda i: (mod_row(i), 0, 2)),
            pl.BlockSpec((1, D_MODEL), lambda i: (0, 0)),
            pl.BlockSpec((N_BRANCH, MIX_W, D_MODEL), lambda i: (0, 0, 0)),
            pl.BlockSpec((D_MODEL, D_MODEL), lambda i: (0, 0)),
        ],
        out_specs=pl.BlockSpec((tm, D_MODEL), row),
        out_shape=jax.ShapeDtypeStruct((t, D_MODEL), F32),
        compiler_params=_cparams("arbitrary"),
        name="merge",
    )(x, ya, yb, yc, gl, mod3, lw["g_post1"], lw["w_branch"], lw["w_out"])


def _mlp_kernel(x_ref, sh_ref, sc_ref, g2_ref, gpre_ref, gpost_ref, w1_ref, w2_ref, o_ref, *, ff_chunk):
    x = x_ref[...]
    hb = (_rms(x, gpre_ref[...]) * (1.0 + sc_ref[0]) + sh_ref[0]).astype(BF16)
    f = None
    for c0 in range(0, D_FF, ff_chunk):
        u = jnp.maximum(_dot(hb, w1_ref[:, c0:c0 + ff_chunk]), 0.0)
        part = _dot((u * u).astype(BF16), w2_ref[c0:c0 + ff_chunk, :])
        f = part if f is None else f + part
    o_ref[...] = x + g2_ref[0] * _rms(f, gpost_ref[...])


def _mlp_call(x, mod3, mod_row, lw, *, tm):
    t = x.shape[0]
    row = lambda i: (i, 0)
    modspec = lambda col: pl.BlockSpec((1, 1, D_MODEL), lambda i: (mod_row(i), 0, col))
    return pl.pallas_call(
        functools.partial(_mlp_kernel, ff_chunk=1024),
        grid=(t // tm,),
        in_specs=[
            pl.BlockSpec((tm, D_MODEL), row),
            modspec(3), modspec(4), modspec(5),
            pl.BlockSpec((1, D_MODEL), lambda i: (0, 0)),
            pl.BlockSpec((1, D_MODEL), lambda i: (0, 0)),
            pl.BlockSpec((D_MODEL, D_FF), lambda i: (0, 0)),
            pl.BlockSpec((D_FF, D_MODEL), lambda i: (0, 0)),
        ],
        out_specs=pl.BlockSpec((tm, D_MODEL), row),
        out_shape=jax.ShapeDtypeStruct((t, D_MODEL), F32),
        compiler_params=_cparams("arbitrary"),
        name="mlp",
    )(x, mod3, mod3, mod3, lw["g_pre2"], lw["g_post2"], lw["w_ff1"], lw["w_ff2"])


def _rope_tables(n):
    pos = jnp.arange(n)
    rows = (pos // GRID_W).astype(F32)[:, None]
    cols = (pos % GRID_W).astype(F32)[:, None]

    def pattern(dim):
        half = dim // 4
        inv = jnp.power(ROPE_BASE, -jnp.arange(half, dtype=F32) * (2.0 / (dim // 2)))
        ar, ac = rows * inv, cols * inv
        c = jnp.concatenate([jnp.cos(ar), jnp.cos(ar), jnp.cos(ac), jnp.cos(ac)], axis=1)
        s = jnp.concatenate([-jnp.sin(ar), jnp.sin(ar), -jnp.sin(ac), jnp.sin(ac)], axis=1)
        return c, s

    c64, s64 = pattern(A_HEAD_DIM)
    ca = jnp.tile(c64, (1, LANES // A_HEAD_DIM))
    sa = jnp.tile(s64, (1, LANES // A_HEAD_DIM))
    c32, s32 = pattern(B_ROPE)
    tail = HEAD_PAD - B_QK_DIM
    cb = jnp.concatenate([jnp.ones((n, B_NOPE), F32), c32, jnp.ones((n, tail), F32)], axis=1)
    sb = jnp.concatenate([jnp.zeros((n, B_NOPE), F32), s32, jnp.zeros((n, tail), F32)], axis=1)
    return (ca, sa, cb, sb, (ca * A_QSCALE).T, (sa * A_QSCALE).T, (cb * B_QSCALE).T, (sb * B_QSCALE).T)


def _prep_weights(p):
    w_in = p["w_in"]
    qw = A_HEADS * A_HEAD_DIM
    o_k = qw
    o_v = o_k + LANES
    o_cq = o_v + LANES
    o_ckv = o_cq + B_Q_LORA
    o_kr = o_ckv + B_KV_LORA
    o_xc = o_kr + B_ROPE
    o_gl = o_xc + 2 * C_WIDTH
    w_q, w_k, w_v = w_in[:, :, :o_k], w_in[:, :, o_k:o_v], w_in[:, :, o_v:o_cq]
    w_cq, w_ckv, w_kr = w_in[:, :, o_cq:o_ckv], w_in[:, :, o_ckv:o_kr], w_in[:, :, o_kr:o_xc]
    kr_placed = jnp.pad(w_kr, ((0, 0), (0, 0), (B_NOPE, HEAD_PAD - B_QK_DIM)))
    pad_heads = lambda w, dh: jnp.pad(
        w.reshape(DEPTH, w.shape[1], B_HEADS, dh), ((0, 0), (0, 0), (0, 0), (0, HEAD_PAD - dh))
    ).reshape(DEPTH, w.shape[1], B_HEADS * HEAD_PAD)
    tr = lambda w: jnp.swapaxes(w, 1, 2)

    eye = jnp.eye(C_BLOCKS, dtype=F32)
    ng = C_WIDTH // LANES

    def gate_groups(w):
        full = (w[:, :, :, None, :] * eye[None, :, None, :, None]).reshape(DEPTH, C_WIDTH, C_WIDTH)
        full = full.reshape(DEPTH, ng, LANES, ng, LANES)
        return jnp.stack([full[:, g, :, g, :] for g in range(ng)], axis=1)

    w_ra, w_ri, b_ra, b_ri = p["w_ra"], p["w_ri"], p["b_ra"], p["b_ri"]
    wgate = jnp.concatenate([gate_groups(w_ra[:, 0]), gate_groups(w_ri[:, 0]),
                             gate_groups(w_ra[:, 1]), gate_groups(w_ri[:, 1])], axis=3)
    grp = lambda b: b.reshape(DEPTH, ng, 1, LANES)
    bgate = jnp.concatenate([grp(b_ra[:, 0]), grp(b_ri[:, 0]), grp(b_ra[:, 1]), grp(b_ri[:, 1])], axis=3)

    row = lambda g: g[:, None, :]
    return dict(
        g_pre1=row(p["g_pre1"]), g_post1=row(p["g_post1"]), g_pre2=row(p["g_pre2"]), g_post2=row(p["g_post2"]),
        g_q=row(p["g_q"]), g_kv=row(p["g_kv"]),
        wqat=tr(w_q).astype(BF16),
        wkv=jnp.concatenate([w_k, w_v], axis=2).astype(BF16),
        wvat=tr(w_v).astype(BF16),
        wb=jnp.concatenate([w_cq, kr_placed, w_ckv], axis=2).astype(BF16),
        wl=w_in[:, :, o_xc:o_gl].astype(BF16), wg=w_in[:, :, o_gl:].astype(BF16),
        wqt=tr(pad_heads(p["w_q_up"], B_QK_DIM)).astype(BF16),
        wuk=pad_heads(p["w_uk"], B_NOPE).astype(BF16),
        wuvt=tr(p["w_uv"]).astype(BF16),
        w_conv=p["w_conv"], b_conv=row(p["b_conv"]), lam=p["lam"],
        wgate=(0.5 * wgate).astype(BF16), bgate=0.5 * bgate,
        w_branch=p["w_branch"].astype(BF16), w_out=p["w_out"].astype(BF16),
        w_ff1=p["w_ff1"].astype(BF16), w_ff2=p["w_ff2"].astype(BF16),
    )


def _pick_tile(n, target, multiple=1):
    t = max(multiple, min(n, target) // multiple * multiple)
    while n % t:
        t -= multiple
    return t


def kernel(x_prompt, x_sample, cache_attn_k, cache_attn_v, cache_mla_ckv, cache_mla_krope, state_lru, c, c_ctx, w_ada, b_ada, g_pre1, g_post1, g_pre2, g_post2, w_in, sink, g_q, w_q_up, g_kv, w_uk, w_uv, w_conv, b_conv, lam, w_ra, b_ra, w_ri, b_ri, w_branch, w_out, w_ff1, w_ff2):
    nbc, seq, _ = x_prompt.shape
    nbl, n_lat, _ = x_sample.shape
    past = cache_attn_k.shape[2]
    assert n_lat % GRID_W == 0 and n_lat >= GQA_Q_TILE + 2 * A_WINDOW and n_lat % GQA_Q_TILE == 0
    assert seq % (SUBLANES * SUBLANES) == 0 and n_lat % (SUBLANES * SUBLANES) == 0

    weights = _prep_weights(dict(
        g_pre1=g_pre1, g_post1=g_post1, g_pre2=g_pre2, g_post2=g_post2, w_in=w_in, g_q=g_q,
        w_q_up=w_q_up, g_kv=g_kv, w_uk=w_uk, w_uv=w_uv, w_conv=w_conv, b_conv=b_conv, lam=lam,
        w_ra=w_ra, b_ra=b_ra, w_ri=w_ri, b_ri=b_ri, w_branch=w_branch, w_out=w_out,
        w_ff1=w_ff1, w_ff2=w_ff2))
    tables = _rope_tables(n_lat)

    n_mod = -(-(nbl + 1) // SUBLANES) * SUBLANES
    cmat = jnp.zeros((n_mod, D_MODEL), F32).at[:nbl].set(c).at[nbl].set(c_ctx)
    mod3 = _ada_call(cmat, w_ada, b_ada).reshape(DEPTH * n_mod, 1, N_ADA * D_MODEL)

    place = jnp.pad(jnp.eye(B_ROPE, dtype=BF16), ((0, 0), (B_NOPE, HEAD_PAD - B_QK_DIM)))
    kc_mla, vtc_mla = _mla_cache_call(cache_mla_ckv, cache_mla_krope, weights["wuk"], weights["wuvt"], place)
    kc_a = jnp.swapaxes(cache_attn_k, 2, 3).astype(BF16)
    ones_rows = jnp.zeros((nbl, DEPTH, A_KV_HEADS, V_ROWS - A_HEAD_DIM, past), BF16).at[:, :, :, 0].set(1)
    vtc_a = jnp.concatenate(
        [jnp.transpose(cache_attn_v, (0, 1, 3, 4, 2)).astype(BF16), ones_rows], axis=3)

    tm_p = _pick_tile(seq, 512)
    tm_pw = _pick_tile(nbc * seq, 512, SUBLANES)
    tm_s = _pick_tile(n_lat, 512)
    y_p = x_prompt.reshape(nbc * seq, D_MODEL)
    y_s = x_sample.reshape(nbl * n_lat, D_MODEL)
    zero_state = jnp.zeros((nbc, 2, C_WIDTH), F32)
    ks_l, vs_l, ckv_l, kr_l, st_l = [], [], [], [], []
    for l in range(DEPTH):
        lw = {k: v[l] for k, v in weights.items()}
        row_p = lambda i, l=l: l * n_mod + nbl
        row_s = lambda i, l=l: l * n_mod + (i * tm_s) // n_lat

        (qat, ka, vat, qbt, kb, vbt, xc, gc, gl, kaf, vaf, ckvn, krf) = _inproj_call(
            y_p, mod3, row_p, lw, None, latent=False, seq_len=seq, tm=tm_p)
        r3 = lambda a: a.reshape(nbc, seq, a.shape[-1])
        ya = _gqa_call(sink[l], qat, ka, vat)
        yb = _mla_call(qbt, r3(kb), vbt, tq=seq, tk=seq)
        yc, st = _lru_call(r3(xc), r3(gc), lw, zero_state)
        f2 = lambda a: a.reshape(nbc * seq, a.shape[-1])
        y_p = _merge_call(y_p, f2(ya), f2(yb), f2(yc), gl, mod3, row_p, lw, tm=tm_pw)
        y_p = _mlp_call(y_p, mod3, row_p, lw, tm=tm_pw)
        ks_l.append(kaf.reshape(nbc, seq, A_KV_HEADS, A_HEAD_DIM))
        vs_l.append(vaf.reshape(nbc, seq, A_KV_HEADS, A_HEAD_DIM))
        ckv_l.append(ckvn.reshape(nbc, seq, B_KV_LORA))
        kr_l.append(krf.reshape(nbc, seq, B_ROPE))
        st_l.append(st)

        (qat, ka, vat, qbt, kb, vbt, xc, gc, gl) = _inproj_call(
            y_s, mod3, row_s, lw, tables, latent=True, seq_len=n_lat, tm=tm_s)
        r3 = lambda a: a.reshape(nbl, n_lat, a.shape[-1])
        ya = _gqa_call(sink[l], qat, ka, vat, kc_a, vtc_a, layer=l)
        yb = _mla_call(qbt, r3(kb), vbt, kc_mla, vtc_mla, layer=l, tq=_pick_tile(n_lat, 512, LANES),
                       tk=_pick_tile(math.gcd(n_lat, past), 256, LANES))
        yc, _ = _lru_call(r3(xc), r3(gc), lw, state_lru[:, l])
        f2 = lambda a: a.reshape(nbl * n_lat, a.shape[-1])
        y_s = _merge_call(y_s, f2(ya), f2(yb), f2(yc), gl, mod3, row_s, lw, tm=tm_s)
        y_s = _mlp_call(y_s, mod3, row_s, lw, tm=tm_s)

    return (y_p.reshape(nbc, seq, D_MODEL), y_s.reshape(nbl, n_lat, D_MODEL),
            jnp.stack(ks_l, axis=1), jnp.stack(vs_l, axis=1), jnp.stack(ckv_l, axis=1),
            jnp.stack(kr_l, axis=1), jnp.stack(st_l, axis=1))
```

```python
import functools
import math

import jax
import jax.numpy as jnp
from jax import lax
from jax.experimental import pallas as pl
from jax.experimental.pallas import tpu as pltpu

F32 = jnp.float32
BF16 = jnp.bfloat16

D_MODEL = 1024
DEPTH = 4
GRID_W = 64
Q_BLOCK = 128
A_HEADS = 8
A_KV_HEADS = 2
A_HEAD_DIM = 64
A_GROUP = A_HEADS // A_KV_HEADS
A_WINDOW = 128
B_HEADS = 8
B_Q_LORA = 384
B_KV_LORA = 256
B_NOPE = 64
B_ROPE = 32
B_QK_DIM = B_NOPE + B_ROPE
B_V_DIM = 64
C_WIDTH = 512
C_BLOCKS = 8
C_BLOCK_W = C_WIDTH // C_BLOCKS
C_CONV = 4
C_RG = 8.0
MIX_W = 512
N_BRANCH = 3
D_FF = 4 * D_MODEL
N_ADA = 6
ROPE_BASE = 10000.0
EPS = 1e-6
NEG_INF = -1e30
LOG2E = math.log2(math.e)

LANES = 128
SUBLANES = 8
HEAD_PAD = 128
GQA_Q_TILE = 128
V_ROWS = 80
VMEM_LIMIT = 56 * 1024 * 1024
A_QSCALE = A_HEAD_DIM ** -0.5 * LOG2E
B_QSCALE = B_QK_DIM ** -0.5 * LOG2E


def _cparams(*sem):
    return pltpu.CompilerParams(dimension_semantics=sem, vmem_limit_bytes=VMEM_LIMIT)


def _dot(a, b):
    return jnp.dot(a, b, preferred_element_type=F32)


def _dot_nt(a, b):
    return lax.dot_general(a, b, (((1,), (1,)), ((), ())), preferred_element_type=F32)


def _rms(x, g):
    return x * lax.rsqrt(jnp.mean(x * x, axis=-1, keepdims=True) + EPS) * g


def _rope(x, c, s, half, axis):
    pos = lax.broadcasted_iota(jnp.int32, x.shape, axis)
    lo = (pos & (2 * half - 1)) < half
    partner = jnp.where(lo, pltpu.roll(x, LANES - half, axis), pltpu.roll(x, half, axis))
    return x * c + partner * s


def _ada_kernel(c_ref, w_ref, b_ref, o_ref):
    c = c_ref[...]
    a = (c * jax.nn.sigmoid(c)).astype(BF16)
    o_ref[0] = _dot(a, w_ref[0].astype(BF16)) + b_ref[0]


def _ada_call(cmat, w_ada, b_ada):
    nb = cmat.shape[0]
    n_out = N_ADA * D_MODEL
    tn = 1536
    return pl.pallas_call(
        _ada_kernel,
        grid=(DEPTH, n_out // tn),
        in_specs=[
            pl.BlockSpec((nb, D_MODEL), lambda l, j: (0, 0)),
            pl.BlockSpec((1, D_MODEL, tn), lambda l, j: (l, 0, j)),
            pl.BlockSpec((1, 1, tn), lambda l, j: (l, 0, j)),
        ],
        out_specs=pl.BlockSpec((1, nb, tn), lambda l, j: (l, 0, j)),
        out_shape=jax.ShapeDtypeStruct((DEPTH, nb, n_out), F32),
        compiler_params=_cparams("arbitrary", "arbitrary"),
        name="ada_mod",
    )(cmat, w_ada, b_ada.reshape(DEPTH, 1, n_out))


def _ones_row_block(cols):
    r = lax.broadcasted_iota(jnp.int32, (V_ROWS - B_V_DIM, cols), 0)
    return jnp.where(r == 0, 1.0, 0.0).astype(BF16)


def _inproj_kernel(*refs, latent):
    (x_ref, sh_ref, sc_ref, gpre_ref, wqat_ref, wkv_ref, wvat_ref, wb_ref, wl_ref, wg_ref,
     gq_ref, wqt_ref, gkv_ref, wuk_ref, wuvt_ref) = refs[:15]
    refs = refs[15:]
    if latent:
        ca_ref, sa_ref, cb_ref, sb_ref, caqt_ref, saqt_ref, cbqt_ref, sbqt_ref = refs[:8]
        refs = refs[8:]
        qat_ref, ka_ref, vat_ref, qbt_ref, kb_ref, vbt_ref, xc_ref, gc_ref, gl_ref = refs
    else:
        (qat_ref, ka_ref, vat_ref, qbt_ref, kb_ref, vbt_ref, xc_ref, gc_ref, gl_ref,
         kaf_ref, vaf_ref, ckvn_ref, krf_ref) = refs

    x = x_ref[...]
    h = _rms(x, gpre_ref[...]) * (1.0 + sc_ref[0]) + sh_ref[0]
    hb = h.astype(BF16)

    tm = x.shape[0]
    ones_blk = _ones_row_block(tm)
    qat = _dot_nt(wqat_ref[...], hb)
    pkv = _dot(hb, wkv_ref[...])
    ka = pkv[:, :LANES]
    vat = _dot_nt(wvat_ref[...], hb)
    if latent:
        caqt, saqt = caqt_ref[...], saqt_ref[...]
        qat = jnp.concatenate(
            [_rope(qat[j * LANES:(j + 1) * LANES, :], caqt, saqt, 16, 0)
             for j in range(A_HEADS * A_HEAD_DIM // LANES)], axis=0)
        ka = _rope(ka, ca_ref[...], sa_ref[...], 16, 1)
    else:
        qat = qat * A_QSCALE
        kaf_ref[...] = ka
        vaf_ref[...] = pkv[:, LANES:]
    qat_ref[0] = qat.astype(BF16)
    for g in range(A_KV_HEADS):
        ka_ref[0, g] = ka[:, g * A_HEAD_DIM:(g + 1) * A_HEAD_DIM].astype(BF16)
        vat_ref[0, g, 0:A_HEAD_DIM, :] = vat[g * A_HEAD_DIM:(g + 1) * A_HEAD_DIM, :].astype(BF16)
        vat_ref[0, g, A_HEAD_DIM:V_ROWS, :] = ones_blk

    pb = _dot(hb, wb_ref[...])
    cq = pb[:, :B_Q_LORA]
    krp = pb[:, B_Q_LORA:B_Q_LORA + HEAD_PAD]
    ckv = pb[:, B_Q_LORA + HEAD_PAD:]
    qt = _dot_nt(wqt_ref[...], _rms(cq, gq_ref[...]).astype(BF16))
    ckvn = _rms(ckv, gkv_ref[...])
    ckvb = ckvn.astype(BF16)
    kn = _dot(ckvb, wuk_ref[...])
    vt = _dot_nt(wuvt_ref[...], ckvb)
    for hd in range(B_HEADS):
        vbt_ref[0, hd, 0:B_V_DIM, :] = vt[hd * B_V_DIM:(hd + 1) * B_V_DIM, :].astype(BF16)
        vbt_ref[0, hd, B_V_DIM:V_ROWS, :] = ones_blk
    if latent:
        cbqt, sbqt = cbqt_ref[...], sbqt_ref[...]
        qt = jnp.concatenate(
            [_rope(qt[j * HEAD_PAD:(j + 1) * HEAD_PAD, :], cbqt, sbqt, 8, 0) for j in range(B_HEADS)], axis=0)
        krp = _rope(krp, cb_ref[...], sb_ref[...], 8, 1)
    else:
        qt = qt * B_QSCALE
        ckvn_ref[...] = ckvn
        krf_ref[...] = krp[:, B_NOPE:B_QK_DIM]
    qbt_ref[0] = qt.astype(BF16)
    kb_ref[...] = jnp.concatenate(
        [kn[:, j * HEAD_PAD:(j + 1) * HEAD_PAD] + krp for j in range(B_HEADS)], axis=1).astype(BF16)

    pc = _dot(hb, wl_ref[...])
    xc_ref[...] = pc[:, :C_WIDTH]
    gc_ref[...] = jax.nn.gelu(pc[:, C_WIDTH:])
    gl_ref[...] = _dot(hb, wg_ref[...])


def _inproj_call(x, mod3, mod_row, lw, tables, *, latent, seq_len, tm):
    t = x.shape[0]
    nb = t // seq_len
    tps = seq_len // tm
    row = lambda i: (i, 0)
    const = lambda i: (0, 0)
    modspec = lambda col: pl.BlockSpec((1, 1, D_MODEL), lambda i: (mod_row(i), 0, col))
    wspec = lambda w: pl.BlockSpec(w.shape, const, pipeline_mode=pl.Buffered(1))
    sfx = "_lat" if latent else "_ctx"
    weights = [lw["g_pre1"], lw["wqat"], lw["wkv"], lw["wvat"], lw["wb"], lw["wl"], lw["wg"],
               lw["g_q"], lw["wqt"], lw["g_kv"], lw["wuk"], lw["wuvt"]]
    in_specs = [pl.BlockSpec((tm, D_MODEL), row), modspec(0), modspec(1)] + [wspec(w) for w in weights]
    args = [x, mod3, mod3] + weights
    if latent:
        rspec = pl.BlockSpec((tm, LANES), lambda i: (i % tps, 0))
        cspec = pl.BlockSpec((LANES, tm), lambda i: (0, i % tps))
        in_specs += [rspec] * 4 + [cspec] * 4
        args += list(tables)
    rows2 = lambda w, dt: (jax.ShapeDtypeStruct((t, w), dt), pl.BlockSpec((tm, w), row))
    kw = B_HEADS * HEAD_PAD
    qw = A_HEADS * A_HEAD_DIM
    outs = [
        (jax.ShapeDtypeStruct((nb, qw, seq_len), BF16),
         pl.BlockSpec((1, qw, tm), lambda i: (i // tps, 0, i % tps))),
        (jax.ShapeDtypeStruct((nb, A_KV_HEADS, seq_len, A_HEAD_DIM), BF16),
         pl.BlockSpec((1, A_KV_HEADS, tm, A_HEAD_DIM), lambda i: (i // tps, 0, i % tps, 0))),
        (jax.ShapeDtypeStruct((nb, A_KV_HEADS, V_ROWS, seq_len), BF16),
         pl.BlockSpec((1, A_KV_HEADS, V_ROWS, tm), lambda i: (i // tps, 0, 0, i % tps))),
        (jax.ShapeDtypeStruct((nb, kw, seq_len), BF16),
         pl.BlockSpec((1, kw, tm), lambda i: (i // tps, 0, i % tps))),
        rows2(kw, BF16),
        (jax.ShapeDtypeStruct((nb, B_HEADS, V_ROWS, seq_len), BF16),
         pl.BlockSpec((1, B_HEADS, V_ROWS, tm), lambda i: (i // tps, 0, 0, i % tps))),
        rows2(C_WIDTH, F32), rows2(C_WIDTH, F32), rows2(N_BRANCH * D_MODEL, F32),
    ]
    if not latent:
        outs += [rows2(LANES, F32), rows2(LANES, F32), rows2(B_KV_LORA, F32), rows2(B_ROPE, F32)]
    return pl.pallas_call(
        functools.partial(_inproj_kernel, latent=latent),
        grid=(t // tm,),
        in_specs=in_specs,
        out_specs=[o[1] for o in outs],
        out_shape=[o[0] for o in outs],
        compiler_params=_cparams("arbitrary"),
        name="inproj" + sfx,
    )(*args)


def _mla_cache_kernel(ckv_ref, kr_ref, wuk_ref, wuvt_ref, place_ref, k_ref, vt_ref):
    ckv = ckv_ref[0, 0].astype(BF16)
    kn = _dot(ckv, wuk_ref[0])
    krp = _dot(kr_ref[0, 0].astype(BF16), place_ref[...])
    k_ref[0, 0] = jnp.concatenate(
        [kn[:, j * HEAD_PAD:(j + 1) * HEAD_PAD] + krp for j in range(B_HEADS)], axis=1).astype(BF16)
    vt = _dot_nt(wuvt_ref[0], ckv)
    ones_blk = _ones_row_block(ckv.shape[0])
    for hd in range(B_HEADS):
        vt_ref[0, 0, hd, 0:B_V_DIM, :] = vt[hd * B_V_DIM:(hd + 1) * B_V_DIM, :].astype(BF16)
        vt_ref[0, 0, hd, B_V_DIM:V_ROWS, :] = ones_blk


def _mla_cache_call(cache_ckv, cache_krope, wuk, wuvt, place):
    nb, _, p, _ = cache_ckv.shape
    kw, vw = B_HEADS * HEAD_PAD, B_HEADS * B_V_DIM
    return pl.pallas_call(
        _mla_cache_kernel,
        grid=(DEPTH, nb),
        in_specs=[
            pl.BlockSpec((1, 1, p, B_KV_LORA), lambda l, b: (b, l, 0, 0)),
            pl.BlockSpec((1, 1, p, B_ROPE), lambda l, b: (b, l, 0, 0)),
            pl.BlockSpec((1, B_KV_LORA, kw), lambda l, b: (l, 0, 0)),
            pl.BlockSpec((1, vw, B_KV_LORA), lambda l, b: (l, 0, 0)),
            pl.BlockSpec((B_ROPE, HEAD_PAD), lambda l, b: (0, 0)),
        ],
        out_specs=[
            pl.BlockSpec((1, 1, p, kw), lambda l, b: (b, l, 0, 0)),
            pl.BlockSpec((1, 1, B_HEADS, V_ROWS, p), lambda l, b: (b, l, 0, 0, 0)),
        ],
        out_shape=[jax.ShapeDtypeStruct((nb, DEPTH, p, kw), BF16),
                   jax.ShapeDtypeStruct((nb, DEPTH, B_HEADS, V_ROWS, p), BF16)],
        compiler_params=_cparams("arbitrary", "arbitrary"),
        name="mla_cache_kv",
    )(cache_ckv, cache_krope, wuk, wuvt, place)


def _gqa_group_qt(qt_tile, g):
    return jnp.concatenate([qt_tile(slice(hd * A_HEAD_DIM, (hd + 1) * A_HEAD_DIM))
                            for hd in range(g * A_GROUP, (g + 1) * A_GROUP)], axis=1)


def _gqa_group_out(sink_ref, g, nq, s_l, vt_l, s_c=None, vt_c=None, valid=None):
    sink2 = jnp.concatenate([jnp.full((1, nq), sink_ref[hd] * LOG2E, F32)
                             for hd in range(g * A_GROUP, (g + 1) * A_GROUP)], axis=1)
    m = sink2
    if s_c is not None:
        m = jnp.maximum(m, jnp.max(s_c, axis=0, keepdims=True))
    if valid is not None:
        s_l = jnp.where(valid, s_l, NEG_INF)
    m = jnp.maximum(m, jnp.max(s_l, axis=0, keepdims=True))
    acc = _dot(vt_l, jnp.exp2(s_l - m).astype(BF16))
    if s_c is not None:
        acc = acc + _dot(vt_c, jnp.exp2(s_c - m).astype(BF16))
    den = acc[A_HEAD_DIM:A_HEAD_DIM + 1, :] + jnp.exp2(sink2 - m)
    o_t = acc[:A_HEAD_DIM, :] / den
    return [o_t[:, j * nq:(j + 1) * nq] for j in range(A_GROUP)]


def _gqa_ctx_kernel(sink_ref, qt_ref, k_ref, vt_ref, o_ref):
    nq = qt_ref.shape[2]
    scores = [_dot(k_ref[0, g], _gqa_group_qt(lambda r: qt_ref[0, r, :], g)) for g in range(A_KV_HEADS)]
    heads_t = []
    for g in range(A_KV_HEADS):
        heads_t += _gqa_group_out(sink_ref, g, nq, scores[g], vt_ref[0, g])
    o_ref[0] = jnp.concatenate(heads_t, axis=0).T.astype(BF16)


def _gqa_lat_kernel(sink_ref, qt_ref, k_ref, vt_ref, kc_ref, vtc_ref, o_ref, sl0, sl1, sc0, sc1, *, seq_len):
    nq = GQA_Q_TILE
    win = nq + 2 * A_WINDOW
    cols = A_GROUP * nq
    n_blk = seq_len // nq
    s_loc, s_ctx = (sl0, sl1), (sc0, sc1)
    step = pl.program_id(1)

    def window_start(blk):
        return pl.multiple_of(jnp.clip(blk * nq - A_WINDOW, 0, seq_len - win), A_WINDOW)

    def scores(blk, slot):
        q0 = pl.multiple_of(blk * nq, nq)
        start = window_start(blk)
        for g in range(A_KV_HEADS):
            qgt = _gqa_group_qt(lambda r: qt_ref[0, r, pl.ds(q0, nq)], g)
            s_loc[slot][g] = _dot(k_ref[0, g, pl.ds(start, win), :], qgt)
            s_ctx[slot][g] = _dot(kc_ref[0, 0, g], qgt)

    def finish(blk, slot, half):
        start = window_start(blk)
        kpos = start + lax.broadcasted_iota(jnp.int32, (win, cols), 0)
        qpos = blk * nq + (lax.broadcasted_iota(jnp.int32, (win, cols), 1) & (nq - 1))
        valid = jnp.abs(qpos - kpos) <= A_WINDOW
        heads_t = []
        for g in range(A_KV_HEADS):
            heads_t += _gqa_group_out(sink_ref, g, nq, s_loc[slot][g], vt_ref[0, g, :, pl.ds(start, win)],
                                      s_ctx[slot][g], vtc_ref[0, 0, g], valid)
        o_ref[0, half * nq:(half + 1) * nq, :] = jnp.concatenate(heads_t, axis=0).T.astype(BF16)

    @pl.when(step == 0)
    def _():
        scores(0, 0)

    blk0 = 2 * step
    scores(blk0 + 1, 1)
    finish(blk0, 0, 0)
    scores(jnp.minimum(blk0 + 2, n_blk - 1), 0)
    finish(blk0 + 1, 1, 1)


def _gqa_call(sink_l, qt, k, vt, kc=None, vtc=None, layer=0):
    nb, qw, n = qt.shape
    latent = kc is not None
    smem = pl.BlockSpec(memory_space=pltpu.SMEM)
    in_specs = [smem,
                pl.BlockSpec((1, qw, n), lambda b, i: (b, 0, 0)),
                pl.BlockSpec((1, A_KV_HEADS, n, A_HEAD_DIM), lambda b, i: (b, 0, 0, 0)),
                pl.BlockSpec((1, A_KV_HEADS, V_ROWS, n), lambda b, i: (b, 0, 0, 0))]
    args = [sink_l, qt, k, vt]
    out_shape = jax.ShapeDtypeStruct((nb, n, qw), BF16)
    if not latent:
        return pl.pallas_call(
            _gqa_ctx_kernel,
            grid=(nb, 1),
            in_specs=in_specs,
            out_specs=pl.BlockSpec((1, n, qw), lambda b, i: (b, 0, 0)),
            out_shape=out_shape,
            compiler_params=_cparams("arbitrary", "arbitrary"),
            name="gqa_ctx",
        )(*args)
    p = kc.shape[3]
    nq = GQA_Q_TILE
    cols = A_GROUP * nq
    in_specs += [pl.BlockSpec((1, 1, A_KV_HEADS, p, A_HEAD_DIM), lambda b, i: (b, layer, 0, 0, 0)),
                 pl.BlockSpec((1, 1, A_KV_HEADS, V_ROWS, p), lambda b, i: (b, layer, 0, 0, 0))]
    args += [kc, vtc]
    s_loc = pltpu.VMEM((A_KV_HEADS, nq + 2 * A_WINDOW, cols), F32)
    s_ctx = pltpu.VMEM((A_KV_HEADS, p, cols), F32)
    return pl.pallas_call(
        functools.partial(_gqa_lat_kernel, seq_len=n),
        grid=(nb, n // (2 * nq)),
        in_specs=in_specs,
        out_specs=pl.BlockSpec((1, 2 * nq, qw), lambda b, i: (b, i, 0)),
        out_shape=out_shape,
        scratch_shapes=[s_loc, s_loc, s_ctx, s_ctx],
        compiler_params=_cparams("arbitrary", "arbitrary"),
        name="gqa_lat",
    )(*args)


def _mla_kernel(*refs, has_ctx, n_own, n_ctx, tk):
    if has_ctx:
        qt_ref, k_ref, vt_ref, kc_ref, vtc_ref, o_ref, m_sc, acc_sc, s0, s1, p0, p1, a0, a1 = refs
    else:
        qt_ref, k_ref, vt_ref, o_ref, m_sc, acc_sc, s0, s1, p0, p1, a0, a1 = refs
    s_buf, p_buf, a_buf = (s0, s1), (p0, p1), (a0, a1)
    m_sc[...] = jnp.full(m_sc.shape, -jnp.inf, F32)
    acc_sc[...] = jnp.zeros(acc_sc.shape, F32)

    def span(c):
        return pl.ds(c * tk, tk) if isinstance(c, int) else pl.ds(pl.multiple_of(c * tk, tk), tk)

    def scores(src, c, hd):
        cols = slice(hd * HEAD_PAD, (hd + 1) * HEAD_PAD)
        k = k_ref[0, span(c), cols] if src == "own" else kc_ref[0, 0, span(c), cols]
        s_buf[hd % 2][...] = _dot(k, qt_ref[0, cols, :])

    def softmax(hd):
        s = s_buf[hd % 2][...]
        m_old = m_sc[hd]
        m_new = jnp.maximum(m_old, jnp.max(s, axis=0, keepdims=True))
        p_buf[hd % 2][...] = jnp.exp2(s - m_new).astype(BF16)
        a_buf[hd % 2][...] = jnp.exp2(m_old - m_new)
        m_sc[hd] = m_new

    def weighted_values(src, c, hd):
        vt = vt_ref[0, hd, :, span(c)] if src == "own" else vtc_ref[0, 0, hd, :, span(c)]
        acc_sc[hd] = a_buf[hd % 2][...] * acc_sc[hd] + _dot(vt, p_buf[hd % 2][...])

    def chunk_steps(cur, nxt):
        for hd in range(B_HEADS):
            if hd + 2 < B_HEADS:
                scores(*cur, hd + 2)
            elif nxt is not None:
                scores(*nxt, hd + 2 - B_HEADS)
            if hd + 1 < B_HEADS:
                softmax(hd + 1)
            elif nxt is not None:
                softmax(0)
            weighted_values(*cur, hd)

    def run(src, count, after):
        def body(c, carry):
            chunk_steps((src, c), (src, c + 1))
            return carry
        lax.fori_loop(0, count - 1, body, 0)
        chunk_steps((src, count - 1), after)

    scores("own", 0, 0)
    scores("own", 0, 1)
    softmax(0)
    if has_ctx:
        run("own", n_own, ("ctx", 0))
        run("ctx", n_ctx, None)
    else:
        run("own", n_own, None)
    outs = []
    for hd in range(B_HEADS):
        acc = acc_sc[hd]
        outs.append(acc[:B_V_DIM, :] / acc[B_V_DIM:B_V_DIM + 1, :])
    o_ref[0] = jnp.concatenate(outs, axis=0).T.astype(BF16)


def _mla_call(qt, k, vt, kc=None, vtc=None, layer=0, *, tq, tk):
    nb, n, kw = k.shape
    vw = B_HEADS * B_V_DIM
    has_ctx = kc is not None
    in_specs = [pl.BlockSpec((1, kw, tq), lambda b, i: (b, 0, i)),
                pl.BlockSpec((1, n, kw), lambda b, i: (b, 0, 0)),
                pl.BlockSpec((1, B_HEADS, V_ROWS, n), lambda b, i: (b, 0, 0, 0))]
    args = [qt, k, vt]
    n_ctx = 0
    if has_ctx:
        p = kc.shape[2]
        n_ctx = p // tk
        in_specs += [pl.BlockSpec((1, 1, p, kw), lambda b, i: (b, layer, 0, 0)),
                     pl.BlockSpec((1, 1, B_HEADS, V_ROWS, p), lambda b, i: (b, layer, 0, 0, 0))]
        args += [kc, vtc]
    return pl.pallas_call(
        functools.partial(_mla_kernel, has_ctx=has_ctx, n_own=n // tk, n_ctx=n_ctx, tk=tk),
        grid=(nb, n // tq),
        in_specs=in_specs,
        out_specs=pl.BlockSpec((1, tq, vw), lambda b, i: (b, i, 0)),
        out_shape=jax.ShapeDtypeStruct((nb, n, vw), BF16),
        scratch_shapes=[pltpu.VMEM((B_HEADS, 1, tq), F32), pltpu.VMEM((B_HEADS, V_ROWS, tq), F32),
                        pltpu.VMEM((tk, tq), F32), pltpu.VMEM((tk, tq), F32),
                        pltpu.VMEM((tk, tq), BF16), pltpu.VMEM((tk, tq), BF16),
                        pltpu.VMEM((1, tq), F32), pltpu.VMEM((1, tq), F32)],
        compiler_params=_cparams("arbitrary", "arbitrary"),
        name="mla_lat" if has_ctx else "mla_ctx",
    )(*args)


def _chunk_pitch(lc):
    return lc if (lc // SUBLANES) % 2 == 1 else lc + SUBLANES


def _lru_kernel(xc_ref, gg_ref, wconv_ref, bconv_ref, lam_ref, wg_ref, bg_ref, h0_ref,
                y_ref, st_ref, xpad, af, uf, ab, ub, *, seq_len, rows, n_seq):
    n = seq_len
    lc = n // SUBLANES
    pitch = _chunk_pitch(lc)
    pad = SUBLANES
    wconv = wconv_ref[...]
    bconv = bconv_ref[...]
    z = -lam_ref[...]
    softplus = jnp.maximum(z, 0.0) + jnp.log1p(jnp.exp(-jnp.abs(z)))
    k1 = (0.5 * C_RG) * softplus
    wg = wg_ref[0]
    bg = bg_ref[0]
    tiny = float(jnp.finfo(F32).tiny)
    xpad[0:pad, :] = jnp.zeros((pad, LANES), F32)
    xpad[pad + n:pad + n + pad, :] = jnp.zeros((pad, LANES), F32)

    def one_sequence(bi):
        xpad[pad:pad + n, :] = xc_ref[bi]
        for r0 in range(0, n, rows):
            dst = (r0 // lc) * pitch + r0 % lc
            xconv = bconv
            for j in range(C_CONV):
                xconv = xconv + xpad[pad + r0 + j - 1:pad + r0 + j - 1 + rows, :] * wconv[j:j + 1, :]
            g = _dot(xconv.astype(BF16), wg) + bg
            xh = 0.5 * xconv
            for d, (a_ref, u_ref) in enumerate(((af, uf), (ab, ub))):
                t_r = jnp.tanh(g[:, (2 * d) * LANES:(2 * d + 1) * LANES])
                t_i = jnp.tanh(g[:, (2 * d + 1) * LANES:(2 * d + 2) * LANES])
                kd = k1[d:d + 1, :]
                w = kd * t_r + kd
                a = jnp.exp2(w * (-LOG2E))
                zz = jnp.tanh(w) * (1.0 + a * a)
                mult = zz * lax.rsqrt(jnp.maximum(zz, tiny))
                a_ref[dst:dst + rows, :] = a
                u_ref[dst:dst + rows, :] = mult * (t_i * xh + xh)

        def scan_body(j, carry):
            hf, pf, hb, pb = carry
            fi = pl.ds(j, SUBLANES, stride=pitch)
            ri = pl.ds(lc - 1 - j, SUBLANES, stride=pitch)
            a_f = af[fi, :]
            a_b = ab[ri, :]
            hf = a_f * hf + uf[fi, :]
            hb = a_b * hb + ub[ri, :]
            pf = a_f * pf
            pb = a_b * pb
            uf[fi, :] = hf
            af[fi, :] = pf
            ub[ri, :] = hb
            ab[ri, :] = pb
            return hf, pf, hb, pb

        zeros = jnp.zeros((SUBLANES, LANES), F32)
        ones = jnp.ones((SUBLANES, LANES), F32)
        hf_end, af_end, hb_end, ab_end = lax.fori_loop(0, lc, scan_body, (zeros, ones, zeros, ones), unroll=8)

        h0 = h0_ref[bi]
        cf = [h0[0:1, :]]
        for s in range(SUBLANES):
            cf.append(af_end[s:s + 1, :] * cf[s] + hf_end[s:s + 1, :])
        cb = [None] * (SUBLANES + 1)
        cb[SUBLANES] = h0[1:2, :]
        for s in range(SUBLANES - 1, -1, -1):
            cb[s] = ab_end[s:s + 1, :] * cb[s + 1] + hb_end[s:s + 1, :]
        st_ref[bi] = jnp.concatenate([cf[SUBLANES], cb[0]], axis=0)

        for s in range(SUBLANES):
            src = slice(s * pitch, s * pitch + lc)
            rs = slice(s * lc, (s + 1) * lc)
            hf = uf[src, :] + af[src, :] * cf[s]
            hb = ub[src, :] + ab[src, :] * cb[s + 1]
            y_ref[bi, rs, :] = (gg_ref[bi, rs, :] * (hf + hb)).astype(BF16)

    if n_seq == 1:
        one_sequence(0)
    else:
        def seq_body(bi, carry):
            one_sequence(bi)
            return carry
        lax.fori_loop(0, n_seq, seq_body, 0)


def _lru_call(xc, gc, lw, h0):
    nb, n, _ = xc.shape
    ng = C_WIDTH // LANES
    lc = n // SUBLANES
    rows = min(lc, 512)
    n_scan = SUBLANES * _chunk_pitch(lc)
    bb = _pick_tile(nb, max(1, 1024 // n))
    seq = lambda b, g: (b, 0, g)
    return pl.pallas_call(
        functools.partial(_lru_kernel, seq_len=n, rows=rows, n_seq=bb),
        grid=(nb // bb, ng),
        in_specs=[
            pl.BlockSpec((bb, n, LANES), seq),
            pl.BlockSpec((bb, n, LANES), seq),
            pl.BlockSpec((C_CONV, LANES), lambda b, g: (0, g)),
            pl.BlockSpec((1, LANES), lambda b, g: (0, g)),
            pl.BlockSpec((2, LANES), lambda b, g: (0, g)),
            pl.BlockSpec((1, LANES, 4 * LANES), lambda b, g: (g, 0, 0)),
            pl.BlockSpec((1, 1, 4 * LANES), lambda b, g: (g, 0, 0)),
            pl.BlockSpec((bb, 2, LANES), seq),
        ],
        out_specs=[pl.BlockSpec((bb, n, LANES), seq), pl.BlockSpec((bb, 2, LANES), seq)],
        out_shape=[jax.ShapeDtypeStruct((nb, n, C_WIDTH), BF16),
                   jax.ShapeDtypeStruct((nb, 2, C_WIDTH), F32)],
        scratch_shapes=[pltpu.VMEM((n + 2 * SUBLANES, LANES), F32)] + [pltpu.VMEM((n_scan, LANES), F32)] * 4,
        compiler_params=_cparams("arbitrary", "arbitrary"),
        name="rglru",
    )(xc, gc, lw["w_conv"], lw["b_conv"], lw["lam"], lw["wgate"], lw["bgate"], h0)


def _merge_kernel(x_ref, ya_ref, yb_ref, yc_ref, gl_ref, g1_ref, gpost_ref, wbr_ref, wo_ref, o_ref):
    m = None
    for nbr, y_ref in enumerate((ya_ref, yb_ref, yc_ref)):
        gate = jax.nn.sigmoid(gl_ref[:, nbr * D_MODEL:(nbr + 1) * D_MODEL])
        term = gate * _dot(y_ref[...], wbr_ref[nbr])
        m = term if m is None else m + term
    o = _dot(m.astype(BF16), wo_ref[...])
    o_ref[...] = x_ref[...] + g1_ref[0] * _rms(o, gpost_ref[...])


def _merge_call(x, ya, yb, yc, gl, mod3, mod_row, lw, *, tm):
    t = x.shape[0]
    row = lambda i: (i, 0)
    return pl.pallas_call(
        _merge_kernel,
        grid=(t // tm,),
        in_specs=[
            pl.BlockSpec((tm, D_MODEL), row),
            pl.BlockSpec((tm, MIX_W), row),
            pl.BlockSpec((tm, MIX_W), row),
            pl.BlockSpec((tm, MIX_W), row),
            pl.BlockSpec((tm, N_BRANCH * D_MODEL), row),
            pl.BlockSpec((1, 1, D_MODEL), lambda i: (mod_row(i), 0, 2)),
            pl.BlockSpec((1, D_MODEL), lambda i: (0, 0)),
            pl.BlockSpec((N_BRANCH, MIX_W, D_MODEL), lambda i: (0, 0, 0)),
            pl.BlockSpec((D_MODEL, D_MODEL), lambda i: (0, 0)),
        ],
        out_specs=pl.BlockSpec((tm, D_MODEL), row),
        out_shape=jax.ShapeDtypeStruct((t, D_MODEL), F32),
        compiler_params=_cparams("arbitrary"),
        name="merge",
    )(x, ya, yb, yc, gl, mod3, lw["g_post1"], lw["w_branch"], lw["w_out"])


def _mlp_kernel(x_ref, sh_ref, sc_ref, g2_ref, gpre_ref, gpost_ref, w1_ref, w2_ref, o_ref, *, ff_chunk):
    x = x_ref[...]
    hb = (_rms(x, gpre_ref[...]) * (1.0 + sc_ref[0]) + sh_ref[0]).astype(BF16)
    f = None
    for c0 in range(0, D_FF, ff_chunk):
        u = jnp.maximum(_dot(hb, w1_ref[:, c0:c0 + ff_chunk]), 0.0)
        part = _dot((u * u).astype(BF16), w2_ref[c0:c0 + ff_chunk, :])
        f = part if f is None else f + part
    o_ref[...] = x + g2_ref[0] * _rms(f, gpost_ref[...])


def _mlp_call(x, mod3, mod_row, lw, *, tm):
    t = x.shape[0]
    row = lambda i: (i, 0)
    modspec = lambda col: pl.BlockSpec((1, 1, D_MODEL), lambda i: (mod_row(i), 0, col))
    return pl.pallas_call(
        functools.partial(_mlp_kernel, ff_chunk=1024),
        grid=(t // tm,),
        in_specs=[
            pl.BlockSpec((tm, D_MODEL), row),
            modspec(3), modspec(4), modspec(5),
            pl.BlockSpec((1, D_MODEL), lambda i: (0, 0)),
            pl.BlockSpec((1, D_MODEL), lambda i: (0, 0)),
            pl.BlockSpec((D_MODEL, D_FF), lambda i: (0, 0)),
            pl.BlockSpec((D_FF, D_MODEL), lambda i: (0, 0)),
        ],
        out_specs=pl.BlockSpec((tm, D_MODEL), row),
        out_shape=jax.ShapeDtypeStruct((t, D_MODEL), F32),
        compiler_params=_cparams("arbitrary"),
        name="mlp",
    )(x, mod3, mod3, mod3, lw["g_pre2"], lw["g_post2"], lw["w_ff1"], lw["w_ff2"])


def _rope_tables(n):
    pos = jnp.arange(n)
    rows = (pos // GRID_W).astype(F32)[:, None]
    cols = (pos % GRID_W).astype(F32)[:, None]

    def pattern(dim):
        half = dim // 4
        inv = jnp.power(ROPE_BASE, -jnp.arange(half, dtype=F32) * (2.0 / (dim // 2)))
        ar, ac = rows * inv, cols * inv
        c = jnp.concatenate([jnp.cos(ar), jnp.cos(ar), jnp.cos(ac), jnp.cos(ac)], axis=1)
        s = jnp.concatenate([-jnp.sin(ar), jnp.sin(ar), -jnp.sin(ac), jnp.sin(ac)], axis=1)
        return c, s

    c64, s64 = pattern(A_HEAD_DIM)
    ca = jnp.tile(c64, (1, LANES // A_HEAD_DIM))
    sa = jnp.tile(s64, (1, LANES // A_HEAD_DIM))
    c32, s32 = pattern(B_ROPE)
    tail = HEAD_PAD - B_QK_DIM
    cb = jnp.concatenate([jnp.ones((n, B_NOPE), F32), c32, jnp.ones((n, tail), F32)], axis=1)
    sb = jnp.concatenate([jnp.zeros((n, B_NOPE), F32), s32, jnp.zeros((n, tail), F32)], axis=1)
    return (ca, sa, cb, sb, (ca * A_QSCALE).T, (sa * A_QSCALE).T, (cb * B_QSCALE).T, (sb * B_QSCALE).T)


def _prep_weights(p):
    w_in = p["w_in"]
    qw = A_HEADS * A_HEAD_DIM
    o_k = qw
    o_v = o_k + LANES
    o_cq = o_v + LANES
    o_ckv = o_cq + B_Q_LORA
    o_kr = o_ckv + B_KV_LORA
    o_xc = o_kr + B_ROPE
    o_gl = o_xc + 2 * C_WIDTH
    w_q, w_k, w_v = w_in[:, :, :o_k], w_in[:, :, o_k:o_v], w_in[:, :, o_v:o_cq]
    w_cq, w_ckv, w_kr = w_in[:, :, o_cq:o_ckv], w_in[:, :, o_ckv:o_kr], w_in[:, :, o_kr:o_xc]
    kr_placed = jnp.pad(w_kr, ((0, 0), (0, 0), (B_NOPE, HEAD_PAD - B_QK_DIM)))
    pad_heads = lambda w, dh: jnp.pad(
        w.reshape(DEPTH, w.shape[1], B_HEADS, dh), ((0, 0), (0, 0), (0, 0), (0, HEAD_PAD - dh))
    ).reshape(DEPTH, w.shape[1], B_HEADS * HEAD_PAD)
    tr = lambda w: jnp.swapaxes(w, 1, 2)

    eye = jnp.eye(C_BLOCKS, dtype=F32)
    ng = C_WIDTH // LANES

    def gate_groups(w):
        full = (w[:, :, :, None, :] * eye[None, :, None, :, None]).reshape(DEPTH, C_WIDTH, C_WIDTH)
        full = full.reshape(DEPTH, ng, LANES, ng, LANES)
        return jnp.stack([full[:, g, :, g, :] for g in range(ng)], axis=1)

    w_ra, w_ri, b_ra, b_ri = p["w_ra"], p["w_ri"], p["b_ra"], p["b_ri"]
    wgate = jnp.concatenate([gate_groups(w_ra[:, 0]), gate_groups(w_ri[:, 0]),
                             gate_groups(w_ra[:, 1]), gate_groups(w_ri[:, 1])], axis=3)
    grp = lambda b: b.reshape(DEPTH, ng, 1, LANES)
    bgate = jnp.concatenate([grp(b_ra[:, 0]), grp(b_ri[:, 0]), grp(b_ra[:, 1]), grp(b_ri[:, 1])], axis=3)

    row = lambda g: g[:, None, :]
    return dict(
        g_pre1=row(p["g_pre1"]), g_post1=row(p["g_post1"]), g_pre2=row(p["g_pre2"]), g_post2=row(p["g_post2"]),
        g_q=row(p["g_q"]), g_kv=row(p["g_kv"]),
        wqat=tr(w_q).astype(BF16),
        wkv=jnp.concatenate([w_k, w_v], axis=2).astype(BF16),
        wvat=tr(w_v).astype(BF16),
        wb=jnp.concatenate([w_cq, kr_placed, w_ckv], axis=2).astype(BF16),
        wl=w_in[:, :, o_xc:o_gl].astype(BF16), wg=w_in[:, :, o_gl:].astype(BF16),
        wqt=tr(pad_heads(p["w_q_up"], B_QK_DIM)).astype(BF16),
        wuk=pad_heads(p["w_uk"], B_NOPE).astype(BF16),
        wuvt=tr(p["w_uv"]).astype(BF16),
        w_conv=p["w_conv"], b_conv=row(p["b_conv"]), lam=p["lam"],
        wgate=(0.5 * wgate).astype(BF16), bgate=0.5 * bgate,
        w_branch=p["w_branch"].astype(BF16), w_out=p["w_out"].astype(BF16),
        w_ff1=p["w_ff1"].astype(BF16), w_ff2=p["w_ff2"].astype(BF16),
    )


def _pick_tile(n, target, multiple=1):
    t = max(multiple, min(n, target) // multiple * multiple)
    while n % t:
        t -= multiple
    return t


def kernel(x_prompt, x_sample, cache_attn_k, cache_attn_v, cache_mla_ckv, cache_mla_krope, state_lru, c, c_ctx, w_ada, b_ada, g_pre1, g_post1, g_pre2, g_post2, w_in, sink, g_q, w_q_up, g_kv, w_uk, w_uv, w_conv, b_conv, lam, w_ra, b_ra, w_ri, b_ri, w_branch, w_out, w_ff1, w_ff2):
    nbc, seq, _ = x_prompt.shape
    nbl, n_lat, _ = x_sample.shape
    past = cache_attn_k.shape[2]
    assert n_lat % GRID_W == 0 and n_lat >= GQA_Q_TILE + 2 * A_WINDOW and n_lat % GQA_Q_TILE == 0
    assert seq % (SUBLANES * SUBLANES) == 0 and n_lat % (SUBLANES * SUBLANES) == 0

    weights = _prep_weights(dict(
        g_pre1=g_pre1, g_post1=g_post1, g_pre2=g_pre2, g_post2=g_post2, w_in=w_in, g_q=g_q,
        w_q_up=w_q_up, g_kv=g_kv, w_uk=w_uk, w_uv=w_uv, w_conv=w_conv, b_conv=b_conv, lam=lam,
        w_ra=w_ra, b_ra=b_ra, w_ri=w_ri, b_ri=b_ri, w_branch=w_branch, w_out=w_out,
        w_ff1=w_ff1, w_ff2=w_ff2))
    tables = _rope_tables(n_lat)

    n_mod = -(-(nbl + 1) // SUBLANES) * SUBLANES
    cmat = jnp.zeros((n_mod, D_MODEL), F32).at[:nbl].set(c).at[nbl].set(c_ctx)
    mod3 = _ada_call(cmat, w_ada, b_ada).reshape(DEPTH * n_mod, 1, N_ADA * D_MODEL)

    place = jnp.pad(jnp.eye(B_ROPE, dtype=BF16), ((0, 0), (B_NOPE, HEAD_PAD - B_QK_DIM)))
    kc_mla, vtc_mla = _mla_cache_call(cache_mla_ckv, cache_mla_krope, weights["wuk"], weights["wuvt"], place)
    kc_a = jnp.swapaxes(cache_attn_k, 2, 3).astype(BF16)
    ones_rows = jnp.zeros((nbl, DEPTH, A_KV_HEADS, V_ROWS - A_HEAD_DIM, past), BF16).at[:, :, :, 0].set(1)
    vtc_a = jnp.concatenate(
        [jnp.transpose(cache_attn_v, (0, 1, 3, 4, 2)).astype(BF16), ones_rows], axis=3)

    tm_p = _pick_tile(seq, 512)
    tm_pw = _pick_tile(nbc * seq, 512, SUBLANES)
    tm_s = _pick_tile(n_lat, 512)
    y_p = x_prompt.reshape(nbc * seq, D_MODEL)
    y_s = x_sample.reshape(nbl * n_lat, D_MODEL)
    zero_state = jnp.zeros((nbc, 2, C_WIDTH), F32)
    ks_l, vs_l, ckv_l, kr_l, st_l = [], [], [], [], []
    for l in range(DEPTH):
        lw = {k: v[l] for k, v in weights.items()}
        row_p = lambda i, l=l: l * n_mod + nbl
        row_s = lambda i, l=l: l * n_mod + (i * tm_s) // n_lat

        (qat, ka, vat, qbt, kb, vbt, xc, gc, gl, kaf, vaf, ckvn, krf) = _inproj_call(
            y_p, mod3, row_p, lw, None, latent=False, seq_len=seq, tm=tm_p)
        r3 = lambda a: a.reshape(nbc, seq, a.shape[-1])
        ya = _gqa_call(sink[l], qat, ka, vat)
        yb = _mla_call(qbt, r3(kb), vbt, tq=seq, tk=seq)
        yc, st = _lru_call(r3(xc), r3(gc), lw, zero_state)
        f2 = lambda a: a.reshape(nbc * seq, a.shape[-1])
        y_p = _merge_call(y_p, f2(ya), f2(yb), f2(yc), gl, mod3, row_p, lw, tm=tm_pw)
        y_p = _mlp_call(y_p, mod3, row_p, lw, tm=tm_pw)
        ks_l.append(kaf.reshape(nbc, seq, A_KV_HEADS, A_HEAD_DIM))
        vs_l.append(vaf.reshape(nbc, seq, A_KV_HEADS, A_HEAD_DIM))
        ckv_l.append(ckvn.reshape(nbc, seq, B_KV_LORA))
        kr_l.append(krf.reshape(nbc, seq, B_ROPE))
        st_l.append(st)

        (qat, ka, vat, qbt, kb, vbt, xc, gc, gl) = _inproj_call(
            y_s, mod3, row_s, lw, tables, latent=True, seq_len=n_lat, tm=tm_s)
        r3 = lambda a: a.reshape(nbl, n_lat, a.shape[-1])
        ya = _gqa_call(sink[l], qat, ka, vat, kc_a, vtc_a, layer=l)
        yb = _mla_call(qbt, r3(kb), vbt, kc_mla, vtc_mla, layer=l, tq=_pick_tile(n_lat, 512, LANES),
                       tk=_pick_tile(math.gcd(n_lat, past), 256, LANES))
        yc, _ = _lru_call(r3(xc), r3(gc), lw, state_lru[:, l])
        f2 = lambda a: a.reshape(nbl * n_lat, a.shape[-1])
        y_s = _merge_call(y_s, f2(ya), f2(yb), f2(yc), gl, mod3, row_s, lw, tm=tm_s)
        y_s = _mlp_call(y_s, mod3, row_s, lw, tm=tm_s)

    return (y_p.reshape(nbc, seq, D_MODEL), y_s.reshape(nbl, n_lat, D_MODEL),
            jnp.stack(ks_l, axis=1), jnp.stack(vs_l, axis=1), jnp.stack(ckv_l, axis=1),
            jnp.stack(kr_l, axis=1), jnp.stack(st_l, axis=1))
```

```python
import functools
import math

import jax
import jax.numpy as jnp
from jax import lax
from jax.experimental import pallas as pl
from jax.experimental.pallas import tpu as pltpu

F32 = jnp.float32
BF16 = jnp.bfloat16

D_MODEL = 1024
DEPTH = 4
GRID_W = 64
Q_BLOCK = 128
A_HEADS = 8
A_KV_HEADS = 2
A_HEAD_DIM = 64
A_GROUP = A_HEADS // A_KV_HEADS
A_WINDOW = 128
B_HEADS = 8
B_Q_LORA = 384
B_KV_LORA = 256
B_NOPE = 64
B_ROPE = 32
B_QK_DIM = B_NOPE + B_ROPE
B_V_DIM = 64
C_WIDTH = 512
C_BLOCKS = 8
C_BLOCK_W = C_WIDTH // C_BLOCKS
C_CONV = 4
C_RG = 8.0
MIX_W = 512
N_BRANCH = 3
D_FF = 4 * D_MODEL
N_ADA = 6
ROPE_BASE = 10000.0
EPS = 1e-6
NEG_INF = -1e30
LOG2E = math.log2(math.e)

LANES = 128
SUBLANES = 8
HEAD_PAD = 128
GQA_Q_TILE = 128
V_ROWS = 80
VMEM_LIMIT = 56 * 1024 * 1024
A_QSCALE = A_HEAD_DIM ** -0.5 * LOG2E
B_QSCALE = B_QK_DIM ** -0.5 * LOG2E


def _cparams(*sem):
    return pltpu.CompilerParams(dimension_semantics=sem, vmem_limit_bytes=VMEM_LIMIT)


def _dot(a, b):
    return jnp.dot(a, b, preferred_element_type=F32)


def _dot_nt(a, b):
    return lax.dot_general(a, b, (((1,), (1,)), ((), ())), preferred_element_type=F32)


def _rms(x, g):
    return x * lax.rsqrt(jnp.mean(x * x, axis=-1, keepdims=True) + EPS) * g


def _rope(x, c, s, half, axis):
    pos = lax.broadcasted_iota(jnp.int32, x.shape, axis)
    lo = (pos & (2 * half - 1)) < half
    partner = jnp.where(lo, pltpu.roll(x, LANES - half, axis), pltpu.roll(x, half, axis))
    return x * c + partner * s


def _ada_kernel(c_ref, w_ref, b_ref, o_ref):
    c = c_ref[...]
    a = (c * jax.nn.sigmoid(c)).astype(BF16)
    o_ref[0] = _dot(a, w_ref[0].astype(BF16)) + b_ref[0]


def _ada_call(cmat, w_ada, b_ada):
    nb = cmat.shape[0]
    n_out = N_ADA * D_MODEL
    tn = 1536
    return pl.pallas_call(
        _ada_kernel,
        grid=(DEPTH, n_out // tn),
        in_specs=[
            pl.BlockSpec((nb, D_MODEL), lambda l, j: (0, 0)),
            pl.BlockSpec((1, D_MODEL, tn), lambda l, j: (l, 0, j)),
            pl.BlockSpec((1, 1, tn), lambda l, j: (l, 0, j)),
        ],
        out_specs=pl.BlockSpec((1, nb, tn), lambda l, j: (l, 0, j)),
        out_shape=jax.ShapeDtypeStruct((DEPTH, nb, n_out), F32),
        compiler_params=_cparams("arbitrary", "arbitrary"),
        name="ada_mod",
    )(cmat, w_ada, b_ada.reshape(DEPTH, 1, n_out))


def _ones_row_block(cols):
    r = lax.broadcasted_iota(jnp.int32, (V_ROWS - B_V_DIM, cols), 0)
    return jnp.where(r == 0, 1.0, 0.0).astype(BF16)


def _inproj_kernel(*refs, latent):
    (x_ref, sh_ref, sc_ref, gpre_ref, wqat_ref, wkv_ref, wvat_ref, wb_ref, wl_ref, wg_ref,
     gq_ref, wqt_ref, gkv_ref, wuk_ref, wuvt_ref) = refs[:15]
    refs = refs[15:]
    if latent:
        ca_ref, sa_ref, cb_ref, sb_ref, caqt_ref, saqt_ref, cbqt_ref, sbqt_ref = refs[:8]
        refs = refs[8:]
        qat_ref, ka_ref, vat_ref, qbt_ref, kb_ref, vbt_ref, xc_ref, gc_ref, gl_ref = refs
    else:
        (qat_ref, ka_ref, vat_ref, qbt_ref, kb_ref, vbt_ref, xc_ref, gc_ref, gl_ref,
         kaf_ref, vaf_ref, ckvn_ref, krf_ref) = refs

    x = x_ref[...]
    h = _rms(x, gpre_ref[...]) * (1.0 + sc_ref[0]) + sh_ref[0]
    hb = h.astype(BF16)

    tm = x.shape[0]
    ones_blk = _ones_row_block(tm)
    qat = _dot_nt(wqat_ref[...], hb)
    pkv = _dot(hb, wkv_ref[...])
    ka = pkv[:, :LANES]
    vat = _dot_nt(wvat_ref[...], hb)
    if latent:
        caqt, saqt = caqt_ref[...], saqt_ref[...]
        qat = jnp.concatenate(
            [_rope(qat[j * LANES:(j + 1) * LANES, :], caqt, saqt, 16, 0)
             for j in range(A_HEADS * A_HEAD_DIM // LANES)], axis=0)
        ka = _rope(ka, ca_ref[...], sa_ref[...], 16, 1)
    else:
        qat = qat * A_QSCALE
        kaf_ref[...] = ka
        vaf_ref[...] = pkv[:, LANES:]
    qat_ref[0] = qat.astype(BF16)
    for g in range(A_KV_HEADS):
        ka_ref[0, g] = ka[:, g * A_HEAD_DIM:(g + 1) * A_HEAD_DIM].astype(BF16)
        vat_ref[0, g, 0:A_HEAD_DIM, :] = vat[g * A_HEAD_DIM:(g + 1) * A_HEAD_DIM, :].astype(BF16)
        vat_ref[0, g, A_HEAD_DIM:V_ROWS, :] = ones_blk

    pb = _dot(hb, wb_ref[...])
    cq = pb[:, :B_Q_LORA]
    krp = pb[:, B_Q_LORA:B_Q_LORA + HEAD_PAD]
    ckv = pb[:, B_Q_LORA + HEAD_PAD:]
    qt = _dot_nt(wqt_ref[...], _rms(cq, gq_ref[...]).astype(BF16))
    ckvn = _rms(ckv, gkv_ref[...])
    ckvb = ckvn.astype(BF16)
    kn = _dot(ckvb, wuk_ref[...])
    vt = _dot_nt(wuvt_ref[...], ckvb)
    for hd in range(B_HEADS):
        vbt_ref[0, hd, 0:B_V_DIM, :] = vt[hd * B_V_DIM:(hd + 1) * B_V_DIM, :].astype(BF16)
        vbt_ref[0, hd, B_V_DIM:V_ROWS, :] = ones_blk
    if latent:
        cbqt, sbqt = cbqt_ref[...], sbqt_ref[...]
        qt = jnp.concatenate(
            [_rope(qt[j * HEAD_PAD:(j + 1) * HEAD_PAD, :], cbqt, sbqt, 8, 0) for j in range(B_HEADS)], axis=0)
        krp = _rope(krp, cb_ref[...], sb_ref[...], 8, 1)
    else:
        qt = qt * B_QSCALE
        ckvn_ref[...] = ckvn
        krf_ref[...] = krp[:, B_NOPE:B_QK_DIM]
    qbt_ref[0] = qt.astype(BF16)
    kb_ref[...] = jnp.concatenate(
        [kn[:, j * HEAD_PAD:(j + 1) * HEAD_PAD] + krp for j in range(B_HEADS)], axis=1).astype(BF16)

    pc = _dot(hb, wl_ref[...])
    xc_ref[...] = pc[:, :C_WIDTH]
    gc_ref[...] = jax.nn.gelu(pc[:, C_WIDTH:])
    gl_ref[...] = _dot(hb, wg_ref[...]).astype(BF16)


def _inproj_call(x, mod3, mod_row, lw, tables, *, latent, seq_len, tm):
    t = x.shape[0]
    nb = t // seq_len
    tps = seq_len // tm
    row = lambda i: (i, 0)
    const = lambda i: (0, 0)
    modspec = lambda col: pl.BlockSpec((1, 1, D_MODEL), lambda i: (mod_row(i), 0, col))
    wspec = lambda w: pl.BlockSpec(w.shape, const, pipeline_mode=pl.Buffered(1))
    sfx = "_lat" if latent else "_ctx"
    weights = [lw["g_pre1"], lw["wqat"], lw["wkv"], lw["wvat"], lw["wb"], lw["wl"], lw["wg"],
               lw["g_q"], lw["wqt"], lw["g_kv"], lw["wuk"], lw["wuvt"]]
    in_specs = [pl.BlockSpec((tm, D_MODEL), row), modspec(0), modspec(1)] + [wspec(w) for w in weights]
    args = [x, mod3, mod3] + weights
    if latent:
        rspec = pl.BlockSpec((tm, LANES), lambda i: (i % tps, 0))
        cspec = pl.BlockSpec((LANES, tm), lambda i: (0, i % tps))
        in_specs += [rspec] * 4 + [cspec] * 4
        args += list(tables)
    rows2 = lambda w, dt: (jax.ShapeDtypeStruct((t, w), dt), pl.BlockSpec((tm, w), row))
    kw = B_HEADS * HEAD_PAD
    qw = A_HEADS * A_HEAD_DIM
    outs = [
        (jax.ShapeDtypeStruct((nb, qw, seq_len), BF16),
         pl.BlockSpec((1, qw, tm), lambda i: (i // tps, 0, i % tps))),
        (jax.ShapeDtypeStruct((nb, A_KV_HEADS, seq_len, A_HEAD_DIM), BF16),
         pl.BlockSpec((1, A_KV_HEADS, tm, A_HEAD_DIM), lambda i: (i // tps, 0, i % tps, 0))),
        (jax.ShapeDtypeStruct((nb, A_KV_HEADS, V_ROWS, seq_len), BF16),
         pl.BlockSpec((1, A_KV_HEADS, V_ROWS, tm), lambda i: (i // tps, 0, 0, i % tps))),
        (jax.ShapeDtypeStruct((nb, kw, seq_len), BF16),
         pl.BlockSpec((1, kw, tm), lambda i: (i // tps, 0, i % tps))),
        rows2(kw, BF16),
        (jax.ShapeDtypeStruct((nb, B_HEADS, V_ROWS, seq_len), BF16),
         pl.BlockSpec((1, B_HEADS, V_ROWS, tm), lambda i: (i // tps, 0, 0, i % tps))),
        rows2(C_WIDTH, F32), rows2(C_WIDTH, F32), rows2(N_BRANCH * D_MODEL, BF16),
    ]
    if not latent:
        outs += [rows2(LANES, F32), rows2(LANES, F32), rows2(B_KV_LORA, F32), rows2(B_ROPE, F32)]
    return pl.pallas_call(
        functools.partial(_inproj_kernel, latent=latent),
        grid=(t // tm,),
        in_specs=in_specs,
        out_specs=[o[1] for o in outs],
        out_shape=[o[0] for o in outs],
        compiler_params=_cparams("arbitrary"),
        name="inproj" + sfx,
    )(*args)


def _mla_cache_kernel(ckv_ref, kr_ref, wuk_ref, wuvt_ref, place_ref, k_ref, vt_ref):
    ckv = ckv_ref[0, 0].astype(BF16)
    kn = _dot(ckv, wuk_ref[0])
    krp = _dot(kr_ref[0, 0].astype(BF16), place_ref[...])
    k_ref[0, 0] = jnp.concatenate(
        [kn[:, j * HEAD_PAD:(j + 1) * HEAD_PAD] + krp for j in range(B_HEADS)], axis=1).astype(BF16)
    vt = _dot_nt(wuvt_ref[0], ckv)
    ones_blk = _ones_row_block(ckv.shape[0])
    for hd in range(B_HEADS):
        vt_ref[0, 0, hd, 0:B_V_DIM, :] = vt[hd * B_V_DIM:(hd + 1) * B_V_DIM, :].astype(BF16)
        vt_ref[0, 0, hd, B_V_DIM:V_ROWS, :] = ones_blk


def _mla_cache_call(cache_ckv, cache_krope, wuk, wuvt, place):
    nb, _, p, _ = cache_ckv.shape
    kw, vw = B_HEADS * HEAD_PAD, B_HEADS * B_V_DIM
    return pl.pallas_call(
        _mla_cache_kernel,
        grid=(DEPTH, nb),
        in_specs=[
            pl.BlockSpec((1, 1, p, B_KV_LORA), lambda l, b: (b, l, 0, 0)),
            pl.BlockSpec((1, 1, p, B_ROPE), lambda l, b: (b, l, 0, 0)),
            pl.BlockSpec((1, B_KV_LORA, kw), lambda l, b: (l, 0, 0)),
            pl.BlockSpec((1, vw, B_KV_LORA), lambda l, b: (l, 0, 0)),
            pl.BlockSpec((B_ROPE, HEAD_PAD), lambda l, b: (0, 0)),
        ],
        out_specs=[
            pl.BlockSpec((1, 1, p, kw), lambda l, b: (b, l, 0, 0)),
            pl.BlockSpec((1, 1, B_HEADS, V_ROWS, p), lambda l, b: (b, l, 0, 0, 0)),
        ],
        out_shape=[jax.ShapeDtypeStruct((nb, DEPTH, p, kw), BF16),
                   jax.ShapeDtypeStruct((nb, DEPTH, B_HEADS, V_ROWS, p), BF16)],
        compiler_params=_cparams("arbitrary", "arbitrary"),
        name="mla_cache_kv",
    )(cache_ckv, cache_krope, wuk, wuvt, place)


def _gqa_group_qt(qt_tile, g):
    return jnp.concatenate([qt_tile(slice(hd * A_HEAD_DIM, (hd + 1) * A_HEAD_DIM))
                            for hd in range(g * A_GROUP, (g + 1) * A_GROUP)], axis=1)


def _gqa_group_out(sink_ref, g, nq, s_l, vt_l, s_c=None, vt_c=None, valid=None):
    sink2 = jnp.concatenate([jnp.full((1, nq), sink_ref[hd] * LOG2E, F32)
                             for hd in range(g * A_GROUP, (g + 1) * A_GROUP)], axis=1)
    m = sink2
    if s_c is not None:
        m = jnp.maximum(m, jnp.max(s_c, axis=0, keepdims=True))
    if valid is not None:
        s_l = jnp.where(valid, s_l, NEG_INF)
    m = jnp.maximum(m, jnp.max(s_l, axis=0, keepdims=True))
    acc = _dot(vt_l, jnp.exp2(s_l - m).astype(BF16))
    if s_c is not None:
        acc = acc + _dot(vt_c, jnp.exp2(s_c - m).astype(BF16))
    den = acc[A_HEAD_DIM:A_HEAD_DIM + 1, :] + jnp.exp2(sink2 - m)
    o_t = acc[:A_HEAD_DIM, :] / den
    return [o_t[:, j * nq:(j + 1) * nq] for j in range(A_GROUP)]


def _gqa_ctx_kernel(sink_ref, qt_ref, k_ref, vt_ref, o_ref):
    nq = qt_ref.shape[2]
    scores = [_dot(k_ref[0, g], _gqa_group_qt(lambda r: qt_ref[0, r, :], g)) for g in range(A_KV_HEADS)]
    heads_t = []
    for g in range(A_KV_HEADS):
        heads_t += _gqa_group_out(sink_ref, g, nq, scores[g], vt_ref[0, g])
    o_ref[0] = jnp.concatenate(heads_t, axis=0).T.astype(BF16)


def _gqa_lat_kernel(sink_ref, qt_ref, k_ref, vt_ref, kc_ref, vtc_ref, o_ref, sl0, sl1, sc0, sc1, *, seq_len):
    nq = GQA_Q_TILE
    win = nq + 2 * A_WINDOW
    cols = A_GROUP * nq
    n_blk = seq_len // nq
    s_loc, s_ctx = (sl0, sl1), (sc0, sc1)
    step = pl.program_id(1)

    def window_start(blk):
        return pl.multiple_of(jnp.clip(blk * nq - A_WINDOW, 0, seq_len - win), A_WINDOW)

    def scores(blk, slot):
        q0 = pl.multiple_of(blk * nq, nq)
        start = window_start(blk)
        for g in range(A_KV_HEADS):
            qgt = _gqa_group_qt(lambda r: qt_ref[0, r, pl.ds(q0, nq)], g)
            s_loc[slot][g] = _dot(k_ref[0, g, pl.ds(start, win), :], qgt)
            s_ctx[slot][g] = _dot(kc_ref[0, 0, g], qgt)

    def finish(blk, slot, half):
        start = window_start(blk)
        kpos = start + lax.broadcasted_iota(jnp.int32, (win, cols), 0)
        qpos = blk * nq + (lax.broadcasted_iota(jnp.int32, (win, cols), 1) & (nq - 1))
        valid = jnp.abs(qpos - kpos) <= A_WINDOW
        heads_t = []
        for g in range(A_KV_HEADS):
            heads_t += _gqa_group_out(sink_ref, g, nq, s_loc[slot][g], vt_ref[0, g, :, pl.ds(start, win)],
                                      s_ctx[slot][g], vtc_ref[0, 0, g], valid)
        o_ref[0, half * nq:(half + 1) * nq, :] = jnp.concatenate(heads_t, axis=0).T.astype(BF16)

    @pl.when(step == 0)
    def _():
        scores(0, 0)

    blk0 = 2 * step
    scores(blk0 + 1, 1)
    finish(blk0, 0, 0)
    scores(jnp.minimum(blk0 + 2, n_blk - 1), 0)
    finish(blk0 + 1, 1, 1)


def _gqa_call(sink_l, qt, k, vt, kc=None, vtc=None, layer=0):
    nb, qw, n = qt.shape
    latent = kc is not None
    smem = pl.BlockSpec(memory_space=pltpu.SMEM)
    in_specs = [smem,
                pl.BlockSpec((1, qw, n), lambda b, i: (b, 0, 0)),
                pl.BlockSpec((1, A_KV_HEADS, n, A_HEAD_DIM), lambda b, i: (b, 0, 0, 0)),
                pl.BlockSpec((1, A_KV_HEADS, V_ROWS, n), lambda b, i: (b, 0, 0, 0))]
    args = [sink_l, qt, k, vt]
    out_shape = jax.ShapeDtypeStruct((nb, n, qw), BF16)
    if not latent:
        return pl.pallas_call(
            _gqa_ctx_kernel,
            grid=(nb, 1),
            in_specs=in_specs,
            out_specs=pl.BlockSpec((1, n, qw), lambda b, i: (b, 0, 0)),
            out_shape=out_shape,
            compiler_params=_cparams("arbitrary", "arbitrary"),
            name="gqa_ctx",
        )(*args)
    p = kc.shape[3]
    nq = GQA_Q_TILE
    cols = A_GROUP * nq
    in_specs += [pl.BlockSpec((1, 1, A_KV_HEADS, p, A_HEAD_DIM), lambda b, i: (b, layer, 0, 0, 0)),
                 pl.BlockSpec((1, 1, A_KV_HEADS, V_ROWS, p), lambda b, i: (b, layer, 0, 0, 0))]
    args += [kc, vtc]
    s_loc = pltpu.VMEM((A_KV_HEADS, nq + 2 * A_WINDOW, cols), F32)
    s_ctx = pltpu.VMEM((A_KV_HEADS, p, cols), F32)
    return pl.pallas_call(
        functools.partial(_gqa_lat_kernel, seq_len=n),
        grid=(nb, n // (2 * nq)),
        in_specs=in_specs,
        out_specs=pl.BlockSpec((1, 2 * nq, qw), lambda b, i: (b, i, 0)),
        out_shape=out_shape,
        scratch_shapes=[s_loc, s_loc, s_ctx, s_ctx],
        compiler_params=_cparams("arbitrary", "arbitrary"),
        name="gqa_lat",
    )(*args)


def _mla_kernel(*refs, has_ctx, n_own, n_ctx, tk):
    if has_ctx:
        qt_ref, k_ref, vt_ref, kc_ref, vtc_ref, o_ref, m_sc, acc_sc, s0, s1, p0, p1, a0, a1 = refs
    else:
        qt_ref, k_ref, vt_ref, o_ref, m_sc, acc_sc, s0, s1, p0, p1, a0, a1 = refs
    s_buf, p_buf, a_buf = (s0, s1), (p0, p1), (a0, a1)
    m_sc[...] = jnp.full(m_sc.shape, -jnp.inf, F32)
    acc_sc[...] = jnp.zeros(acc_sc.shape, F32)

    def span(c):
        return pl.ds(c * tk, tk) if isinstance(c, int) else pl.ds(pl.multiple_of(c * tk, tk), tk)

    def scores(src, c, hd):
        cols = slice(hd * HEAD_PAD, (hd + 1) * HEAD_PAD)
        k = k_ref[0, span(c), cols] if src == "own" else kc_ref[0, 0, span(c), cols]
        s_buf[hd % 2][...] = _dot(k, qt_ref[0, cols, :])

    def softmax(hd):
        s = s_buf[hd % 2][...]
        m_old = m_sc[hd]
        m_new = jnp.maximum(m_old, jnp.max(s, axis=0, keepdims=True))
        p_buf[hd % 2][...] = jnp.exp2(s - m_new).astype(BF16)
        a_buf[hd % 2][...] = jnp.exp2(m_old - m_new)
        m_sc[hd] = m_new

    def weighted_values(src, c, hd):
        vt = vt_ref[0, hd, :, span(c)] if src == "own" else vtc_ref[0, 0, hd, :, span(c)]
        acc_sc[hd] = a_buf[hd % 2][...] * acc_sc[hd] + _dot(vt, p_buf[hd % 2][...])

    def chunk_steps(cur, nxt):
        for hd in range(B_HEADS):
            if hd + 2 < B_HEADS:
                scores(*cur, hd + 2)
            elif nxt is not None:
                scores(*nxt, hd + 2 - B_HEADS)
            weighted_values(*cur, hd)
            if hd + 1 < B_HEADS:
                softmax(hd + 1)
            elif nxt is not None:
                softmax(0)

    def run(src, count, after):
        def body(c, carry):
            chunk_steps((src, c), (src, c + 1))
            return carry
        lax.fori_loop(0, count - 1, body, 0)
        chunk_steps((src, count - 1), after)

    scores("own", 0, 0)
    scores("own", 0, 1)
    softmax(0)
    if has_ctx:
        run("own", n_own, ("ctx", 0))
        run("ctx", n_ctx, None)
    else:
        run("own", n_own, None)
    outs = []
    for hd in range(B_HEADS):
        acc = acc_sc[hd]
        outs.append(acc[:B_V_DIM, :] / acc[B_V_DIM:B_V_DIM + 1, :])
    o_ref[0] = jnp.concatenate(outs, axis=0).T.astype(BF16)


def _mla_call(qt, k, vt, kc=None, vtc=None, layer=0, *, tq, tk):
    nb, n, kw = k.shape
    vw = B_HEADS * B_V_DIM
    has_ctx = kc is not None
    in_specs = [pl.BlockSpec((1, kw, tq), lambda b, i: (b, 0, i)),
                pl.BlockSpec((1, n, kw), lambda b, i: (b, 0, 0)),
                pl.BlockSpec((1, B_HEADS, V_ROWS, n), lambda b, i: (b, 0, 0, 0))]
    args = [qt, k, vt]
    n_ctx = 0
    if has_ctx:
        p = kc.shape[2]
        n_ctx = p // tk
        in_specs += [pl.BlockSpec((1, 1, p, kw), lambda b, i: (b, layer, 0, 0)),
                     pl.BlockSpec((1, 1, B_HEADS, V_ROWS, p), lambda b, i: (b, layer, 0, 0, 0))]
        args += [kc, vtc]
    return pl.pallas_call(
        functools.partial(_mla_kernel, has_ctx=has_ctx, n_own=n // tk, n_ctx=n_ctx, tk=tk),
        grid=(nb, n // tq),
        in_specs=in_specs,
        out_specs=pl.BlockSpec((1, tq, vw), lambda b, i: (b, i, 0)),
        out_shape=jax.ShapeDtypeStruct((nb, n, vw), BF16),
        scratch_shapes=[pltpu.VMEM((B_HEADS, 1, tq), F32), pltpu.VMEM((B_HEADS, V_ROWS, tq), F32),
                        pltpu.VMEM((tk, tq), F32), pltpu.VMEM((tk, tq), F32),
                        pltpu.VMEM((tk, tq), BF16), pltpu.VMEM((tk, tq), BF16),
                        pltpu.VMEM((1, tq), F32), pltpu.VMEM((1, tq), F32)],
        compiler_params=_cparams("arbitrary", "arbitrary"),
        name="mla_lat" if has_ctx else "mla_ctx",
    )(*args)


def _chunk_pitch(lc):
    return lc if (lc // SUBLANES) % 2 == 1 else lc + SUBLANES


def _lru_kernel(xc_ref, gg_ref, wconv_ref, bconv_ref, lam_ref, wg_ref, bg_ref, h0_ref,
                y_ref, st_ref, xpad, af, uf, ab, ub, *, seq_len, rows, n_seq):
    n = seq_len
    lc = n // SUBLANES
    pitch = _chunk_pitch(lc)
    pad = SUBLANES
    wconv = wconv_ref[...]
    bconv = bconv_ref[...]
    z = -lam_ref[...]
    softplus = jnp.maximum(z, 0.0) + jnp.log1p(jnp.exp(-jnp.abs(z)))
    k1 = (0.5 * C_RG) * softplus
    wg = wg_ref[0]
    bg = bg_ref[0]
    tiny = float(jnp.finfo(F32).tiny)
    xpad[0:pad, :] = jnp.zeros((pad, LANES), F32)
    xpad[pad + n:pad + n + pad, :] = jnp.zeros((pad, LANES), F32)

    def one_sequence(bi):
        xpad[pad:pad + n, :] = xc_ref[bi]
        for r0 in range(0, n, rows):
            dst = (r0 // lc) * pitch + r0 % lc
            xconv = bconv
            for j in range(C_CONV):
                xconv = xconv + xpad[pad + r0 + j - 1:pad + r0 + j - 1 + rows, :] * wconv[j:j + 1, :]
            g = _dot(xconv.astype(BF16), wg) + bg
            xh = 0.5 * xconv
            for d, (a_ref, u_ref) in enumerate(((af, uf), (ab, ub))):
                t_r = jnp.tanh(g[:, (2 * d) * LANES:(2 * d + 1) * LANES])
                t_i = jnp.tanh(g[:, (2 * d + 1) * LANES:(2 * d + 2) * LANES])
                kd = k1[d:d + 1, :]
                w = kd * t_r + kd
                a = jnp.exp2(w * (-LOG2E))
                zz = jnp.tanh(w) * (1.0 + a * a)
                mult = zz * lax.rsqrt(jnp.maximum(zz, tiny))
                a_ref[dst:dst + rows, :] = a
                u_ref[dst:dst + rows, :] = mult * (t_i * xh + xh)

        def scan_body(j, carry):
            hf, pf, hb, pb = carry
            fi = pl.ds(j, SUBLANES, stride=pitch)
            ri = pl.ds(lc - 1 - j, SUBLANES, stride=pitch)
            a_f = af[fi, :]
            a_b = ab[ri, :]
            hf = a_f * hf + uf[fi, :]
            hb = a_b * hb + ub[ri, :]
            pf = a_f * pf
            pb = a_b * pb
            uf[fi, :] = hf
            af[fi, :] = pf
            ub[ri, :] = hb
            ab[ri, :] = pb
            return hf, pf, hb, pb

        zeros = jnp.zeros((SUBLANES, LANES), F32)
        ones = jnp.ones((SUBLANES, LANES), F32)
        hf_end, af_end, hb_end, ab_end = lax.fori_loop(0, lc, scan_body, (zeros, ones, zeros, ones), unroll=8)

        h0 = h0_ref[bi]
        cf = [h0[0:1, :]]
        for s in range(SUBLANES):
            cf.append(af_end[s:s + 1, :] * cf[s] + hf_end[s:s + 1, :])
        cb = [None] * (SUBLANES + 1)
        cb[SUBLANES] = h0[1:2, :]
        for s in range(SUBLANES - 1, -1, -1):
            cb[s] = ab_end[s:s + 1, :] * cb[s + 1] + hb_end[s:s + 1, :]
        st_ref[bi] = jnp.concatenate([cf[SUBLANES], cb[0]], axis=0)

        for s in range(SUBLANES):
            src = slice(s * pitch, s * pitch + lc)
            rs = slice(s * lc, (s + 1) * lc)
            hf = uf[src, :] + af[src, :] * cf[s]
            hb = ub[src, :] + ab[src, :] * cb[s + 1]
            y_ref[bi, rs, :] = (gg_ref[bi, rs, :] * (hf + hb)).astype(BF16)

    if n_seq == 1:
        one_sequence(0)
    else:
        def seq_body(bi, carry):
            one_sequence(bi)
            return carry
        lax.fori_loop(0, n_seq, seq_body, 0)


def _lru_call(xc, gc, lw, h0):
    nb, n, _ = xc.shape
    ng = C_WIDTH // LANES
    lc = n // SUBLANES
    rows = min(lc, 512)
    n_scan = SUBLANES * _chunk_pitch(lc)
    bb = _pick_tile(nb, max(1, 1024 // n))
    seq = lambda b, g: (b, 0, g)
    return pl.pallas_call(
        functools.partial(_lru_kernel, seq_len=n, rows=rows, n_seq=bb),
        grid=(nb // bb, ng),
        in_specs=[
            pl.BlockSpec((bb, n, LANES), seq),
            pl.BlockSpec((bb, n, LANES), seq),
            pl.BlockSpec((C_CONV, LANES), lambda b, g: (0, g)),
            pl.BlockSpec((1, LANES), lambda b, g: (0, g)),
            pl.BlockSpec((2, LANES), lambda b, g: (0, g)),
            pl.BlockSpec((1, LANES, 4 * LANES), lambda b, g: (g, 0, 0)),
            pl.BlockSpec((1, 1, 4 * LANES), lambda b, g: (g, 0, 0)),
            pl.BlockSpec((bb, 2, LANES), seq),
        ],
        out_specs=[pl.BlockSpec((bb, n, LANES), seq), pl.BlockSpec((bb, 2, LANES), seq)],
        out_shape=[jax.ShapeDtypeStruct((nb, n, C_WIDTH), BF16),
                   jax.ShapeDtypeStruct((nb, 2, C_WIDTH), F32)],
        scratch_shapes=[pltpu.VMEM((n + 2 * SUBLANES, LANES), F32)] + [pltpu.VMEM((n_scan, LANES), F32)] * 4,
        compiler_params=_cparams("arbitrary", "arbitrary"),
        name="rglru",
    )(xc, gc, lw["w_conv"], lw["b_conv"], lw["lam"], lw["wgate"], lw["bgate"], h0)


def _merge_kernel(x_ref, ya_ref, yb_ref, yc_ref, gl_ref, g1_ref, gpost_ref, wbr_ref, wo_ref, o_ref):
    m = None
    for nbr, y_ref in enumerate((ya_ref, yb_ref, yc_ref)):
        gate = jax.nn.sigmoid(gl_ref[:, nbr * D_MODEL:(nbr + 1) * D_MODEL].astype(F32))
        term = gate * _dot(y_ref[...], wbr_ref[nbr])
        m = term if m is None else m + term
    o = _dot(m.astype(BF16), wo_ref[...])
    o_ref[...] = x_ref[...] + g1_ref[0] * _rms(o, gpost_ref[...])


def _merge_call(x, ya, yb, yc, gl, mod3, mod_row, lw, *, tm):
    t = x.shape[0]
    row = lambda i: (i, 0)
    return pl.pallas_call(
        _merge_kernel,
        grid=(t // tm,),
        in_specs=[
            pl.BlockSpec((tm, D_MODEL), row),
            pl.BlockSpec((tm, MIX_W), row),
            pl.BlockSpec((tm, MIX_W), row),
            pl.BlockSpec((tm, MIX_W), row),
            pl.BlockSpec((tm, N_BRANCH * D_MODEL), row),
            pl.BlockSpec((1, 1, D_MODEL), lambda i: (mod_row(i), 0, 2)),
            pl.BlockSpec((1, D_MODEL), lambda i: (0, 0)),
            pl.BlockSpec((N_BRANCH, MIX_W, D_MODEL), lambda i: (0, 0, 0)),
            pl.BlockSpec((D_MODEL, D_MODEL), lambda i: (0, 0)),
        ],
        out_specs=pl.BlockSpec((tm, D_MODEL), row),
        out_shape=jax.ShapeDtypeStruct((t, D_MODEL), F32),
        compiler_params=_cparams("arbitrary"),
        name="merge",
    )(x, ya, yb, yc, gl, mod3, lw["g_post1"], lw["w_branch"], lw["w_out"])


def _mlp_kernel(x_ref, sh_ref, sc_ref, g2_ref, gpre_ref, gpost_ref, w1_ref, w2_ref, o_ref, *, ff_chunk):
    x = x_ref[...]
    hb = (_rms(x, gpre_ref[...]) * (1.0 + sc_ref[0]) + sh_ref[0]).astype(BF16)
    f = None
    for c0 in range(0, D_FF, ff_chunk):
        u = jnp.maximum(_dot(hb, w1_ref[:, c0:c0 + ff_chunk]), 0.0)
        part = _dot((u * u).astype(BF16), w2_ref[c0:c0 + ff_chunk, :])
        f = part if f is None else f + part
    o_ref[...] = x + g2_ref[0] * _rms(f, gpost_ref[...])


def _mlp_call(x, mod3, mod_row, lw, *, tm):
    t = x.shape[0]
    row = lambda i: (i, 0)
    modspec = lambda col: pl.BlockSpec((1, 1, D_MODEL), lambda i: (mod_row(i), 0, col))
    return pl.pallas_call(
        functools.partial(_mlp_kernel, ff_chunk=1024),
        grid=(t // tm,),
        in_specs=[
            pl.BlockSpec((tm, D_MODEL), row),
            modspec(3), modspec(4), modspec(5),
            pl.BlockSpec((1, D_MODEL), lambda i: (0, 0)),
            pl.BlockSpec((1, D_MODEL), lambda i: (0, 0)),
            pl.BlockSpec((D_MODEL, D_FF), lambda i: (0, 0)),
            pl.BlockSpec((D_FF, D_MODEL), lambda i: (0, 0)),
        ],
        out_specs=pl.BlockSpec((tm, D_MODEL), row),
        out_shape=jax.ShapeDtypeStruct((t, D_MODEL), F32),
        compiler_params=_cparams("arbitrary"),
        name="mlp",
    )(x, mod3, mod3, mod3, lw["g_pre2"], lw["g_post2"], lw["w_ff1"], lw["w_ff2"])


def _rope_tables(n):
    pos = jnp.arange(n)
    rows = (pos // GRID_W).astype(F32)[:, None]
    cols = (pos % GRID_W).astype(F32)[:, None]

    def pattern(dim):
        half = dim // 4
        inv = jnp.power(ROPE_BASE, -jnp.arange(half, dtype=F32) * (2.0 / (dim // 2)))
        ar, ac = rows * inv, cols * inv
        c = jnp.concatenate([jnp.cos(ar), jnp.cos(ar), jnp.cos(ac), jnp.cos(ac)], axis=1)
        s = jnp.concatenate([-jnp.sin(ar), jnp.sin(ar), -jnp.sin(ac), jnp.sin(ac)], axis=1)
        return c, s

    c64, s64 = pattern(A_HEAD_DIM)
    ca = jnp.tile(c64, (1, LANES // A_HEAD_DIM))
    sa = jnp.tile(s64, (1, LANES // A_HEAD_DIM))
    c32, s32 = pattern(B_ROPE)
    tail = HEAD_PAD - B_QK_DIM
    cb = jnp.concatenate([jnp.ones((n, B_NOPE), F32), c32, jnp.ones((n, tail), F32)], axis=1)
    sb = jnp.concatenate([jnp.zeros((n, B_NOPE), F32), s32, jnp.zeros((n, tail), F32)], axis=1)
    return (ca, sa, cb, sb, (ca * A_QSCALE).T, (sa * A_QSCALE).T, (cb * B_QSCALE).T, (sb * B_QSCALE).T)


def _prep_weights(p):
    w_in = p["w_in"]
    qw = A_HEADS * A_HEAD_DIM
    o_k = qw
    o_v = o_k + LANES
    o_cq = o_v + LANES
    o_ckv = o_cq + B_Q_LORA
    o_kr = o_ckv + B_KV_LORA
    o_xc = o_kr + B_ROPE
    o_gl = o_xc + 2 * C_WIDTH
    w_q, w_k, w_v = w_in[:, :, :o_k], w_in[:, :, o_k:o_v], w_in[:, :, o_v:o_cq]
    w_cq, w_ckv, w_kr = w_in[:, :, o_cq:o_ckv], w_in[:, :, o_ckv:o_kr], w_in[:, :, o_kr:o_xc]
    kr_placed = jnp.pad(w_kr, ((0, 0), (0, 0), (B_NOPE, HEAD_PAD - B_QK_DIM)))
    pad_heads = lambda w, dh: jnp.pad(
        w.reshape(DEPTH, w.shape[1], B_HEADS, dh), ((0, 0), (0, 0), (0, 0), (0, HEAD_PAD - dh))
    ).reshape(DEPTH, w.shape[1], B_HEADS * HEAD_PAD)
    tr = lambda w: jnp.swapaxes(w, 1, 2)

    eye = jnp.eye(C_BLOCKS, dtype=F32)
    ng = C_WIDTH // LANES

    def gate_groups(w):
        full = (w[:, :, :, None, :] * eye[None, :, None, :, None]).reshape(DEPTH, C_WIDTH, C_WIDTH)
        full = full.reshape(DEPTH, ng, LANES, ng, LANES)
        return jnp.stack([full[:, g, :, g, :] for g in range(ng)], axis=1)

    w_ra, w_ri, b_ra, b_ri = p["w_ra"], p["w_ri"], p["b_ra"], p["b_ri"]
    wgate = jnp.concatenate([gate_groups(w_ra[:, 0]), gate_groups(w_ri[:, 0]),
                             gate_groups(w_ra[:, 1]), gate_groups(w_ri[:, 1])], axis=3)
    grp = lambda b: b.reshape(DEPTH, ng, 1, LANES)
    bgate = jnp.concatenate([grp(b_ra[:, 0]), grp(b_ri[:, 0]), grp(b_ra[:, 1]), grp(b_ri[:, 1])], axis=3)

    row = lambda g: g[:, None, :]
    return dict(
        g_pre1=row(p["g_pre1"]), g_post1=row(p["g_post1"]), g_pre2=row(p["g_pre2"]), g_post2=row(p["g_post2"]),
        g_q=row(p["g_q"]), g_kv=row(p["g_kv"]),
        wqat=tr(w_q).astype(BF16),
        wkv=jnp.concatenate([w_k, w_v], axis=2).astype(BF16),
        wvat=tr(w_v).astype(BF16),
        wb=jnp.concatenate([w_cq, kr_placed, w_ckv], axis=2).astype(BF16),
        wl=w_in[:, :, o_xc:o_gl].astype(BF16), wg=w_in[:, :, o_gl:].astype(BF16),
        wqt=tr(pad_heads(p["w_q_up"], B_QK_DIM)).astype(BF16),
        wuk=pad_heads(p["w_uk"], B_NOPE).astype(BF16),
        wuvt=tr(p["w_uv"]).astype(BF16),
        w_conv=p["w_conv"], b_conv=row(p["b_conv"]), lam=p["lam"],
        wgate=(0.5 * wgate).astype(BF16), bgate=0.5 * bgate,
        w_branch=p["w_branch"].astype(BF16), w_out=p["w_out"].astype(BF16),
        w_ff1=p["w_ff1"].astype(BF16), w_ff2=p["w_ff2"].astype(BF16),
    )


def _pick_tile(n, target, multiple=1):
    t = max(multiple, min(n, target) // multiple * multiple)
    while n % t:
        t -= multiple
    return t


def kernel(x_prompt, x_sample, cache_attn_k, cache_attn_v, cache_mla_ckv, cache_mla_krope, state_lru, c, c_ctx, w_ada, b_ada, g_pre1, g_post1, g_pre2, g_post2, w_in, sink, g_q, w_q_up, g_kv, w_uk, w_uv, w_conv, b_conv, lam, w_ra, b_ra, w_ri, b_ri, w_branch, w_out, w_ff1, w_ff2):
    nbc, seq, _ = x_prompt.shape
    nbl, n_lat, _ = x_sample.shape
    past = cache_attn_k.shape[2]
    assert n_lat % GRID_W == 0 and n_lat >= GQA_Q_TILE + 2 * A_WINDOW and n_lat % GQA_Q_TILE == 0
    assert seq % (SUBLANES * SUBLANES) == 0 and n_lat % (SUBLANES * SUBLANES) == 0

    weights = _prep_weights(dict(
        g_pre1=g_pre1, g_post1=g_post1, g_pre2=g_pre2, g_post2=g_post2, w_in=w_in, g_q=g_q,
        w_q_up=w_q_up, g_kv=g_kv, w_uk=w_uk, w_uv=w_uv, w_conv=w_conv, b_conv=b_conv, lam=lam,
        w_ra=w_ra, b_ra=b_ra, w_ri=w_ri, b_ri=b_ri, w_branch=w_branch, w_out=w_out,
        w_ff1=w_ff1, w_ff2=w_ff2))
    tables = _rope_tables(n_lat)

    n_mod = -(-(nbl + 1) // SUBLANES) * SUBLANES
    cmat = jnp.zeros((n_mod, D_MODEL), F32).at[:nbl].set(c).at[nbl].set(c_ctx)
    mod3 = _ada_call(cmat, w_ada, b_ada).reshape(DEPTH * n_mod, 1, N_ADA * D_MODEL)

    place = jnp.pad(jnp.eye(B_ROPE, dtype=BF16), ((0, 0), (B_NOPE, HEAD_PAD - B_QK_DIM)))
    kc_mla, vtc_mla = _mla_cache_call(cache_mla_ckv, cache_mla_krope, weights["wuk"], weights["wuvt"], place)
    kc_a = jnp.swapaxes(cache_attn_k, 2, 3).astype(BF16)
    ones_rows = jnp.zeros((nbl, DEPTH, A_KV_HEADS, V_ROWS - A_HEAD_DIM, past), BF16).at[:, :, :, 0].set(1)
    vtc_a = jnp.concatenate(
        [jnp.transpose(cache_attn_v, (0, 1, 3, 4, 2)).astype(BF16), ones_rows], axis=3)

    tm_p = _pick_tile(seq, 512)
    tm_pw = _pick_tile(nbc * seq, 512, SUBLANES)
    tm_s = _pick_tile(n_lat, 512)
    y_p = x_prompt.reshape(nbc * seq, D_MODEL)
    y_s = x_sample.reshape(nbl * n_lat, D_MODEL)
    zero_state = jnp.zeros((nbc, 2, C_WIDTH), F32)
    ks_l, vs_l, ckv_l, kr_l, st_l = [], [], [], [], []
    for l in range(DEPTH):
        lw = {k: v[l] for k, v in weights.items()}
        row_p = lambda i, l=l: l * n_mod + nbl
        row_s = lambda i, l=l: l * n_mod + (i * tm_s) // n_lat

        (qat, ka, vat, qbt, kb, vbt, xc, gc, gl, kaf, vaf, ckvn, krf) = _inproj_call(
            y_p, mod3, row_p, lw, None, latent=False, seq_len=seq, tm=tm_p)
        r3 = lambda a: a.reshape(nbc, seq, a.shape[-1])
        ya = _gqa_call(sink[l], qat, ka, vat)
        yb = _mla_call(qbt, r3(kb), vbt, tq=seq, tk=seq)
        yc, st = _lru_call(r3(xc), r3(gc), lw, zero_state)
        f2 = lambda a: a.reshape(nbc * seq, a.shape[-1])
        y_p = _merge_call(y_p, f2(ya), f2(yb), f2(yc), gl, mod3, row_p, lw, tm=tm_pw)
        y_p = _mlp_call(y_p, mod3, row_p, lw, tm=tm_pw)
        ks_l.append(kaf.reshape(nbc, seq, A_KV_HEADS, A_HEAD_DIM))
        vs_l.append(vaf.reshape(nbc, seq, A_KV_HEADS, A_HEAD_DIM))
        ckv_l.append(ckvn.reshape(nbc, seq, B_KV_LORA))
        kr_l.append(krf.reshape(nbc, seq, B_ROPE))
        st_l.append(st)

        (qat, ka, vat, qbt, kb, vbt, xc, gc, gl) = _inproj_call(
            y_s, mod3, row_s, lw, tables, latent=True, seq_len=n_lat, tm=tm_s)
        r3 = lambda a: a.reshape(nbl, n_lat, a.shape[-1])
        ya = _gqa_call(sink[l], qat, ka, vat, kc_a, vtc_a, layer=l)
        yb = _mla_call(qbt, r3(kb), vbt, kc_mla, vtc_mla, layer=l, tq=_pick_tile(n_lat, 512, LANES),
                       tk=_pick_tile(math.gcd(n_lat, past), 256, LANES))
        yc, _ = _lru_call(r3(xc), r3(gc), lw, state_lru[:, l])
        f2 = lambda a: a.reshape(nbl * n_lat, a.shape[-1])
        y_s = _merge_call(y_s, f2(ya), f2(yb), f2(yc), gl, mod3, row_s, lw, tm=tm_s)
        y_s = _mlp_call(y_s, mod3, row_s, lw, tm=tm_s)

    return (y_p.reshape(nbc, seq, D_MODEL), y_s.reshape(nbl, n_lat, D_MODEL),
            jnp.stack(ks_l, axis=1), jnp.stack(vs_l, axis=1), jnp.stack(ckv_l, axis=1),
            jnp.stack(kr_l, axis=1), jnp.stack(st_l, axis=1))
```

```python
import functools
import math

import jax
import jax.numpy as jnp
from jax import lax
from jax.experimental import pallas as pl
from jax.experimental.pallas import tpu as pltpu

F32 = jnp.float32
BF16 = jnp.bfloat16

D_MODEL = 1024
DEPTH = 4
GRID_W = 64
Q_BLOCK = 128
A_HEADS = 8
A_KV_HEADS = 2
A_HEAD_DIM = 64
A_GROUP = A_HEADS // A_KV_HEADS
A_WINDOW = 128
B_HEADS = 8
B_Q_LORA = 384
B_KV_LORA = 256
B_NOPE = 64
B_ROPE = 32
B_QK_DIM = B_NOPE + B_ROPE
B_V_DIM = 64
C_WIDTH = 512
C_BLOCKS = 8
C_BLOCK_W = C_WIDTH // C_BLOCKS
C_CONV = 4
C_RG = 8.0
MIX_W = 512
N_BRANCH = 3
D_FF = 4 * D_MODEL
N_ADA = 6
ROPE_BASE = 10000.0
EPS = 1e-6
NEG_INF = -1e30
LOG2E = math.log2(math.e)

LANES = 128
SUBLANES = 8
HEAD_PAD = 128
GQA_Q_TILE = 128
V_ROWS = 80
VMEM_LIMIT = 56 * 1024 * 1024
A_QSCALE = A_HEAD_DIM ** -0.5 * LOG2E
B_QSCALE = B_QK_DIM ** -0.5 * LOG2E


def _cparams(*sem):
    return pltpu.CompilerParams(dimension_semantics=sem, vmem_limit_bytes=VMEM_LIMIT)


def _dot(a, b):
    return jnp.dot(a, b, preferred_element_type=F32)


def _dot_nt(a, b):
    return lax.dot_general(a, b, (((1,), (1,)), ((), ())), preferred_element_type=F32)


def _rms(x, g):
    return x * lax.rsqrt(jnp.mean(x * x, axis=-1, keepdims=True) + EPS) * g


def _rope(x, c, s, half, axis):
    pos = lax.broadcasted_iota(jnp.int32, x.shape, axis)
    lo = (pos & (2 * half - 1)) < half
    partner = jnp.where(lo, pltpu.roll(x, LANES - half, axis), pltpu.roll(x, half, axis))
    return x * c + partner * s


def _ada_kernel(c_ref, w_ref, b_ref, o_ref):
    c = c_ref[...]
    a = (c * jax.nn.sigmoid(c)).astype(BF16)
    o_ref[0] = _dot(a, w_ref[0].astype(BF16)) + b_ref[0]


def _ada_call(cmat, w_ada, b_ada):
    nb = cmat.shape[0]
    n_out = N_ADA * D_MODEL
    tn = 1536
    return pl.pallas_call(
        _ada_kernel,
        grid=(DEPTH, n_out // tn),
        in_specs=[
            pl.BlockSpec((nb, D_MODEL), lambda l, j: (0, 0)),
            pl.BlockSpec((1, D_MODEL, tn), lambda l, j: (l, 0, j)),
            pl.BlockSpec((1, 1, tn), lambda l, j: (l, 0, j)),
        ],
        out_specs=pl.BlockSpec((1, nb, tn), lambda l, j: (l, 0, j)),
        out_shape=jax.ShapeDtypeStruct((DEPTH, nb, n_out), F32),
        compiler_params=_cparams("arbitrary", "arbitrary"),
        name="ada_mod",
    )(cmat, w_ada, b_ada.reshape(DEPTH, 1, n_out))


def _ones_row_block(cols):
    r = lax.broadcasted_iota(jnp.int32, (V_ROWS - B_V_DIM, cols), 0)
    return jnp.where(r == 0, 1.0, 0.0).astype(BF16)


def _inproj_kernel(*refs, latent):
    (x_ref, sh_ref, sc_ref, gpre_ref, wqat_ref, wkv_ref, wvat_ref, wb_ref, wl_ref, wg_ref,
     gq_ref, wqt_ref, gkv_ref, wuk_ref, wuvt_ref) = refs[:15]
    refs = refs[15:]
    if latent:
        ca_ref, sa_ref, cb_ref, sb_ref, caqt_ref, saqt_ref, cbqt_ref, sbqt_ref = refs[:8]
        refs = refs[8:]
        qat_ref, ka_ref, vat_ref, qbt_ref, kb_ref, vbt_ref, xc_ref, gc_ref, gl_ref = refs
    else:
        (qat_ref, ka_ref, vat_ref, qbt_ref, kb_ref, vbt_ref, xc_ref, gc_ref, gl_ref,
         kaf_ref, vaf_ref, ckvn_ref, krf_ref) = refs

    x = x_ref[...]
    h = _rms(x, gpre_ref[...]) * (1.0 + sc_ref[0]) + sh_ref[0]
    hb = h.astype(BF16)

    tm = x.shape[0]
    ones_blk = _ones_row_block(tm)
    qat = _dot_nt(wqat_ref[...], hb)
    pkv = _dot(hb, wkv_ref[...])
    ka = pkv[:, :LANES]
    vat = _dot_nt(wvat_ref[...], hb)
    if latent:
        caqt, saqt = caqt_ref[...], saqt_ref[...]
        qat = jnp.concatenate(
            [_rope(qat[j * LANES:(j + 1) * LANES, :], caqt, saqt, 16, 0)
             for j in range(A_HEADS * A_HEAD_DIM // LANES)], axis=0)
        ka = _rope(ka, ca_ref[...], sa_ref[...], 16, 1)
    else:
        qat = qat * A_QSCALE
        kaf_ref[...] = ka
        vaf_ref[...] = pkv[:, LANES:]
    qat_ref[0] = qat.astype(BF16)
    for g in range(A_KV_HEADS):
        ka_ref[0, g] = ka[:, g * A_HEAD_DIM:(g + 1) * A_HEAD_DIM].astype(BF16)
        vat_ref[0, g, 0:A_HEAD_DIM, :] = vat[g * A_HEAD_DIM:(g + 1) * A_HEAD_DIM, :].astype(BF16)
        vat_ref[0, g, A_HEAD_DIM:V_ROWS, :] = ones_blk

    pb = _dot(hb, wb_ref[...])
    cq = pb[:, :B_Q_LORA]
    krp = pb[:, B_Q_LORA:B_Q_LORA + HEAD_PAD]
    ckv = pb[:, B_Q_LORA + HEAD_PAD:]
    qt = _dot_nt(wqt_ref[...], _rms(cq, gq_ref[...]).astype(BF16))
    ckvn = _rms(ckv, gkv_ref[...])
    ckvb = ckvn.astype(BF16)
    kn = _dot(ckvb, wuk_ref[...])
    vt = _dot_nt(wuvt_ref[...], ckvb)
    for hd in range(B_HEADS):
        vbt_ref[0, hd, 0:B_V_DIM, :] = vt[hd * B_V_DIM:(hd + 1) * B_V_DIM, :].astype(BF16)
        vbt_ref[0, hd, B_V_DIM:V_ROWS, :] = ones_blk
    if latent:
        cbqt, sbqt = cbqt_ref[...], sbqt_ref[...]
        qt = jnp.concatenate(
            [_rope(qt[j * HEAD_PAD:(j + 1) * HEAD_PAD, :], cbqt, sbqt, 8, 0) for j in range(B_HEADS)], axis=0)
        krp = _rope(krp, cb_ref[...], sb_ref[...], 8, 1)
    else:
        qt = qt * B_QSCALE
        ckvn_ref[...] = ckvn
        krf_ref[...] = krp[:, B_NOPE:B_QK_DIM]
    qbt_ref[0] = qt.astype(BF16)
    kb_ref[...] = jnp.concatenate(
        [kn[:, j * HEAD_PAD:(j + 1) * HEAD_PAD] + krp for j in range(B_HEADS)], axis=1).astype(BF16)

    pc = _dot(hb, wl_ref[...])
    xc_ref[...] = pc[:, :C_WIDTH]
    gc_ref[...] = jax.nn.gelu(pc[:, C_WIDTH:])
    gl_ref[...] = _dot(hb, wg_ref[...]).astype(BF16)


def _inproj_call(x, mod3, mod_row, lw, tables, *, latent, seq_len, tm):
    t = x.shape[0]
    nb = t // seq_len
    tps = seq_len // tm
    row = lambda i: (i, 0)
    const = lambda i: (0, 0)
    modspec = lambda col: pl.BlockSpec((1, 1, D_MODEL), lambda i: (mod_row(i), 0, col))
    wspec = lambda w: pl.BlockSpec(w.shape, const, pipeline_mode=pl.Buffered(1))
    sfx = "_lat" if latent else "_ctx"
    weights = [lw["g_pre1"], lw["wqat"], lw["wkv"], lw["wvat"], lw["wb"], lw["wl"], lw["wg"],
               lw["g_q"], lw["wqt"], lw["g_kv"], lw["wuk"], lw["wuvt"]]
    in_specs = [pl.BlockSpec((tm, D_MODEL), row), modspec(0), modspec(1)] + [wspec(w) for w in weights]
    args = [x, mod3, mod3] + weights
    if latent:
        rspec = pl.BlockSpec((tm, LANES), lambda i: (i % tps, 0))
        cspec = pl.BlockSpec((LANES, tm), lambda i: (0, i % tps))
        in_specs += [rspec] * 4 + [cspec] * 4
        args += list(tables)
    rows2 = lambda w, dt: (jax.ShapeDtypeStruct((t, w), dt), pl.BlockSpec((tm, w), row))
    kw = B_HEADS * HEAD_PAD
    qw = A_HEADS * A_HEAD_DIM
    outs = [
        (jax.ShapeDtypeStruct((nb, qw, seq_len), BF16),
         pl.BlockSpec((1, qw, tm), lambda i: (i // tps, 0, i % tps))),
        (jax.ShapeDtypeStruct((nb, A_KV_HEADS, seq_len, A_HEAD_DIM), BF16),
         pl.BlockSpec((1, A_KV_HEADS, tm, A_HEAD_DIM), lambda i: (i // tps, 0, i % tps, 0))),
        (jax.ShapeDtypeStruct((nb, A_KV_HEADS, V_ROWS, seq_len), BF16),
         pl.BlockSpec((1, A_KV_HEADS, V_ROWS, tm), lambda i: (i // tps, 0, 0, i % tps))),
        (jax.ShapeDtypeStruct((nb, kw, seq_len), BF16),
         pl.BlockSpec((1, kw, tm), lambda i: (i // tps, 0, i % tps))),
        rows2(kw, BF16),
        (jax.ShapeDtypeStruct((nb, B_HEADS, V_ROWS, seq_len), BF16),
         pl.BlockSpec((1, B_HEADS, V_ROWS, tm), lambda i: (i // tps, 0, 0, i % tps))),
        rows2(C_WIDTH, F32), rows2(C_WIDTH, F32), rows2(N_BRANCH * D_MODEL, BF16),
    ]
    if not latent:
        outs += [rows2(LANES, F32), rows2(LANES, F32), rows2(B_KV_LORA, F32), rows2(B_ROPE, F32)]
    return pl.pallas_call(
        functools.partial(_inproj_kernel, latent=latent),
        grid=(t // tm,),
        in_specs=in_specs,
        out_specs=[o[1] for o in outs],
        out_shape=[o[0] for o in outs],
        compiler_params=_cparams("arbitrary"),
        name="inproj" + sfx,
    )(*args)


def _mla_cache_kernel(ckv_ref, kr_ref, wuk_ref, wuvt_ref, place_ref, k_ref, vt_ref):
    ckv = ckv_ref[0, 0].astype(BF16)
    kn = _dot(ckv, wuk_ref[0])
    krp = _dot(kr_ref[0, 0].astype(BF16), place_ref[...])
    k_ref[0, 0] = jnp.concatenate(
        [kn[:, j * HEAD_PAD:(j + 1) * HEAD_PAD] + krp for j in range(B_HEADS)], axis=1).astype(BF16)
    vt = _dot_nt(wuvt_ref[0], ckv)
    ones_blk = _ones_row_block(ckv.shape[0])
    for hd in range(B_HEADS):
        vt_ref[0, 0, hd, 0:B_V_DIM, :] = vt[hd * B_V_DIM:(hd + 1) * B_V_DIM, :].astype(BF16)
        vt_ref[0, 0, hd, B_V_DIM:V_ROWS, :] = ones_blk


def _mla_cache_call(cache_ckv, cache_krope, wuk, wuvt, place):
    nb, _, p, _ = cache_ckv.shape
    kw, vw = B_HEADS * HEAD_PAD, B_HEADS * B_V_DIM
    return pl.pallas_call(
        _mla_cache_kernel,
        grid=(DEPTH, nb),
        in_specs=[
            pl.BlockSpec((1, 1, p, B_KV_LORA), lambda l, b: (b, l, 0, 0)),
            pl.BlockSpec((1, 1, p, B_ROPE), lambda l, b: (b, l, 0, 0)),
            pl.BlockSpec((1, B_KV_LORA, kw), lambda l, b: (l, 0, 0)),
            pl.BlockSpec((1, vw, B_KV_LORA), lambda l, b: (l, 0, 0)),
            pl.BlockSpec((B_ROPE, HEAD_PAD), lambda l, b: (0, 0)),
        ],
        out_specs=[
            pl.BlockSpec((1, 1, p, kw), lambda l, b: (b, l, 0, 0)),
            pl.BlockSpec((1, 1, B_HEADS, V_ROWS, p), lambda l, b: (b, l, 0, 0, 0)),
        ],
        out_shape=[jax.ShapeDtypeStruct((nb, DEPTH, p, kw), BF16),
                   jax.ShapeDtypeStruct((nb, DEPTH, B_HEADS, V_ROWS, p), BF16)],
        compiler_params=_cparams("arbitrary", "arbitrary"),
        name="mla_cache_kv",
    )(cache_ckv, cache_krope, wuk, wuvt, place)


def _gqa_group_qt(qt_tile, g):
    return jnp.concatenate([qt_tile(slice(hd * A_HEAD_DIM, (hd + 1) * A_HEAD_DIM))
                            for hd in range(g * A_GROUP, (g + 1) * A_GROUP)], axis=1)


def _gqa_group_out(sink_ref, g, nq, s_l, vt_l, s_c=None, vt_c=None, valid=None):
    sink2 = jnp.concatenate([jnp.full((1, nq), sink_ref[hd] * LOG2E, F32)
                             for hd in range(g * A_GROUP, (g + 1) * A_GROUP)], axis=1)
    m = sink2
    if s_c is not None:
        m = jnp.maximum(m, jnp.max(s_c, axis=0, keepdims=True))
    if valid is not None:
        s_l = jnp.where(valid, s_l, NEG_INF)
    m = jnp.maximum(m, jnp.max(s_l, axis=0, keepdims=True))
    acc = _dot(vt_l, jnp.exp2(s_l - m).astype(BF16))
    if s_c is not None:
        acc = acc + _dot(vt_c, jnp.exp2(s_c - m).astype(BF16))
    den = acc[A_HEAD_DIM:A_HEAD_DIM + 1, :] + jnp.exp2(sink2 - m)
    o_t = acc[:A_HEAD_DIM, :] / den
    return [o_t[:, j * nq:(j + 1) * nq] for j in range(A_GROUP)]


def _gqa_ctx_kernel(sink_ref, qt_ref, k_ref, vt_ref, o_ref):
    nq = qt_ref.shape[2]
    scores = [_dot(k_ref[0, g], _gqa_group_qt(lambda r: qt_ref[0, r, :], g)) for g in range(A_KV_HEADS)]
    heads_t = []
    for g in range(A_KV_HEADS):
        heads_t += _gqa_group_out(sink_ref, g, nq, scores[g], vt_ref[0, g])
    o_ref[0] = jnp.concatenate(heads_t, axis=0).T.astype(BF16)


def _gqa_lat_kernel(sink_ref, qt_ref, k_ref, vt_ref, kc_ref, vtc_ref, o_ref, sl0, sl1, sc0, sc1, *, seq_len):
    nq = GQA_Q_TILE
    win = nq + 2 * A_WINDOW
    cols = A_GROUP * nq
    n_blk = seq_len // nq
    s_loc, s_ctx = (sl0, sl1), (sc0, sc1)
    step = pl.program_id(1)

    def window_start(blk):
        return pl.multiple_of(jnp.clip(blk * nq - A_WINDOW, 0, seq_len - win), A_WINDOW)

    def scores(blk, slot):
        q0 = pl.multiple_of(blk * nq, nq)
        start = window_start(blk)
        for g in range(A_KV_HEADS):
            qgt = _gqa_group_qt(lambda r: qt_ref[0, r, pl.ds(q0, nq)], g)
            s_loc[slot][g] = _dot(k_ref[0, g, pl.ds(start, win), :], qgt)
            s_ctx[slot][g] = _dot(kc_ref[0, 0, g], qgt)

    def finish(blk, slot, half):
        start = window_start(blk)
        kpos = start + lax.broadcasted_iota(jnp.int32, (win, cols), 0)
        qpos = blk * nq + (lax.broadcasted_iota(jnp.int32, (win, cols), 1) & (nq - 1))
        valid = jnp.abs(qpos - kpos) <= A_WINDOW
        heads_t = []
        for g in range(A_KV_HEADS):
            heads_t += _gqa_group_out(sink_ref, g, nq, s_loc[slot][g], vt_ref[0, g, :, pl.ds(start, win)],
                                      s_ctx[slot][g], vtc_ref[0, 0, g], valid)
        o_ref[0, half * nq:(half + 1) * nq, :] = jnp.concatenate(heads_t, axis=0).T.astype(BF16)

    @pl.when(step == 0)
    def _():
        scores(0, 0)

    blk0 = 2 * step
    scores(blk0 + 1, 1)
    finish(blk0, 0, 0)
    scores(jnp.minimum(blk0 + 2, n_blk - 1), 0)
    finish(blk0 + 1, 1, 1)


def _gqa_call(sink_l, qt, k, vt, kc=None, vtc=None, layer=0):
    nb, qw, n = qt.shape
    latent = kc is not None
    smem = pl.BlockSpec(memory_space=pltpu.SMEM)
    in_specs = [smem,
                pl.BlockSpec((1, qw, n), lambda b, i: (b, 0, 0)),
                pl.BlockSpec((1, A_KV_HEADS, n, A_HEAD_DIM), lambda b, i: (b, 0, 0, 0)),
                pl.BlockSpec((1, A_KV_HEADS, V_ROWS, n), lambda b, i: (b, 0, 0, 0))]
    args = [sink_l, qt, k, vt]
    out_shape = jax.ShapeDtypeStruct((nb, n, qw), BF16)
    if not latent:
        return pl.pallas_call(
            _gqa_ctx_kernel,
            grid=(nb, 1),
            in_specs=in_specs,
            out_specs=pl.BlockSpec((1, n, qw), lambda b, i: (b, 0, 0)),
            out_shape=out_shape,
            compiler_params=_cparams("arbitrary", "arbitrary"),
            name="gqa_ctx",
        )(*args)
    p = kc.shape[3]
    nq = GQA_Q_TILE
    cols = A_GROUP * nq
    in_specs += [pl.BlockSpec((1, 1, A_KV_HEADS, p, A_HEAD_DIM), lambda b, i: (b, layer, 0, 0, 0)),
                 pl.BlockSpec((1, 1, A_KV_HEADS, V_ROWS, p), lambda b, i: (b, layer, 0, 0, 0))]
    args += [kc, vtc]
    s_loc = pltpu.VMEM((A_KV_HEADS, nq + 2 * A_WINDOW, cols), F32)
    s_ctx = pltpu.VMEM((A_KV_HEADS, p, cols), F32)
    return pl.pallas_call(
        functools.partial(_gqa_lat_kernel, seq_len=n),
        grid=(nb, n // (2 * nq)),
        in_specs=in_specs,
        out_specs=pl.BlockSpec((1, 2 * nq, qw), lambda b, i: (b, i, 0)),
        out_shape=out_shape,
        scratch_shapes=[s_loc, s_loc, s_ctx, s_ctx],
        compiler_params=_cparams("arbitrary", "arbitrary"),
        name="gqa_lat",
    )(*args)


def _mla_kernel(*refs, has_ctx, n_own, n_ctx, tk):
    if has_ctx:
        qt_ref, k_ref, vt_ref, kc_ref, vtc_ref, o_ref, m_sc, acc_sc, s0, s1, p0, p1, a0, a1 = refs
    else:
        qt_ref, k_ref, vt_ref, o_ref, m_sc, acc_sc, s0, s1, p0, p1, a0, a1 = refs
    s_buf, p_buf, a_buf = (s0, s1), (p0, p1), (a0, a1)
    m_sc[...] = jnp.full(m_sc.shape, -jnp.inf, F32)
    acc_sc[...] = jnp.zeros(acc_sc.shape, F32)

    def span(c):
        return pl.ds(c * tk, tk) if isinstance(c, int) else pl.ds(pl.multiple_of(c * tk, tk), tk)

    def scores(src, c, hd):
        cols = slice(hd * HEAD_PAD, (hd + 1) * HEAD_PAD)
        k = k_ref[0, span(c), cols] if src == "own" else kc_ref[0, 0, span(c), cols]
        s_buf[hd % 2][...] = _dot(k, qt_ref[0, cols, :])

    def softmax(hd):
        s = s_buf[hd % 2][...]
        m_old = m_sc[hd]
        m_new = jnp.maximum(m_old, jnp.max(s, axis=0, keepdims=True))
        p_buf[hd % 2][...] = jnp.exp2(s - m_new).astype(BF16)
        a_buf[hd % 2][...] = jnp.exp2(m_old - m_new)
        m_sc[hd] = m_new

    def weighted_values(src, c, hd):
        vt = vt_ref[0, hd, :, span(c)] if src == "own" else vtc_ref[0, 0, hd, :, span(c)]
        acc_sc[hd] = a_buf[hd % 2][...] * acc_sc[hd] + _dot(vt, p_buf[hd % 2][...])

    def chunk_steps(cur, nxt):
        for hd in range(B_HEADS):
            if hd + 2 < B_HEADS:
                scores(*cur, hd + 2)
            elif nxt is not None:
                scores(*nxt, hd + 2 - B_HEADS)
            weighted_values(*cur, hd)
            if hd + 1 < B_HEADS:
                softmax(hd + 1)
            elif nxt is not None:
                softmax(0)

    def run(src, count, after):
        def body(c, carry):
            chunk_steps((src, c), (src, c + 1))
            return carry
        lax.fori_loop(0, count - 1, body, 0)
        chunk_steps((src, count - 1), after)

    scores("own", 0, 0)
    scores("own", 0, 1)
    softmax(0)
    if has_ctx:
        run("own", n_own, ("ctx", 0))
        run("ctx", n_ctx, None)
    else:
        run("own", n_own, None)
    outs = []
    for hd in range(B_HEADS):
        acc = acc_sc[hd]
        outs.append(acc[:B_V_DIM, :] / acc[B_V_DIM:B_V_DIM + 1, :])
    o_ref[0] = jnp.concatenate(outs, axis=0).T.astype(BF16)


def _mla_call(qt, k, vt, kc=None, vtc=None, layer=0, *, tq, tk):
    nb, n, kw = k.shape
    vw = B_HEADS * B_V_DIM
    has_ctx = kc is not None
    in_specs = [pl.BlockSpec((1, kw, tq), lambda b, i: (b, 0, i)),
                pl.BlockSpec((1, n, kw), lambda b, i: (b, 0, 0)),
                pl.BlockSpec((1, B_HEADS, V_ROWS, n), lambda b, i: (b, 0, 0, 0))]
    args = [qt, k, vt]
    n_ctx = 0
    if has_ctx:
        p = kc.shape[2]
        n_ctx = p // tk
        in_specs += [pl.BlockSpec((1, 1, p, kw), lambda b, i: (b, layer, 0, 0)),
                     pl.BlockSpec((1, 1, B_HEADS, V_ROWS, p), lambda b, i: (b, layer, 0, 0, 0))]
        args += [kc, vtc]
    return pl.pallas_call(
        functools.partial(_mla_kernel, has_ctx=has_ctx, n_own=n // tk, n_ctx=n_ctx, tk=tk),
        grid=(nb, n // tq),
        in_specs=in_specs,
        out_specs=pl.BlockSpec((1, tq, vw), lambda b, i: (b, i, 0)),
        out_shape=jax.ShapeDtypeStruct((nb, n, vw), BF16),
        scratch_shapes=[pltpu.VMEM((B_HEADS, 1, tq), F32), pltpu.VMEM((B_HEADS, V_ROWS, tq), F32),
                        pltpu.VMEM((tk, tq), F32), pltpu.VMEM((tk, tq), F32),
                        pltpu.VMEM((tk, tq), BF16), pltpu.VMEM((tk, tq), BF16),
                        pltpu.VMEM((1, tq), F32), pltpu.VMEM((1, tq), F32)],
        compiler_params=_cparams("arbitrary", "arbitrary"),
        name="mla_lat" if has_ctx else "mla_ctx",
    )(*args)


def _chunk_pitch(lc):
    return lc if (lc // SUBLANES) % 2 == 1 else lc + SUBLANES


def _lru_kernel(xc_ref, gg_ref, wconv_ref, bconv_ref, lam_ref, wg_ref, bg_ref, h0_ref,
                y_ref, st_ref, xpad, af, uf, ab, ub, *, seq_len, rows, n_seq):
    n = seq_len
    lc = n // SUBLANES
    pitch = _chunk_pitch(lc)
    pad = SUBLANES
    wconv = wconv_ref[...]
    bconv = bconv_ref[...]
    z = -lam_ref[...]
    softplus = jnp.maximum(z, 0.0) + jnp.log1p(jnp.exp(-jnp.abs(z)))
    k1 = (0.5 * C_RG) * softplus
    wg = wg_ref[0]
    bg = bg_ref[0]
    tiny = float(jnp.finfo(F32).tiny)
    xpad[0:pad, :] = jnp.zeros((pad, LANES), F32)
    xpad[pad + n:pad + n + pad, :] = jnp.zeros((pad, LANES), F32)

    def one_sequence(bi):
        xpad[pad:pad + n, :] = xc_ref[bi]
        for r0 in range(0, n, rows):
            dst = (r0 // lc) * pitch + r0 % lc
            xconv = bconv
            for j in range(C_CONV):
                xconv = xconv + xpad[pad + r0 + j - 1:pad + r0 + j - 1 + rows, :] * wconv[j:j + 1, :]
            g = _dot(xconv.astype(BF16), wg) + bg
            xh = 0.5 * xconv
            for d, (a_ref, u_ref) in enumerate(((af, uf), (ab, ub))):
                t_r = jnp.tanh(g[:, (2 * d) * LANES:(2 * d + 1) * LANES])
                t_i = jnp.tanh(g[:, (2 * d + 1) * LANES:(2 * d + 2) * LANES])
                kd = k1[d:d + 1, :]
                w = kd * t_r + kd
                a = jnp.exp2(w * (-LOG2E))
                zz = jnp.tanh(w) * (1.0 + a * a)
                mult = zz * lax.rsqrt(jnp.maximum(zz, tiny))
                a_ref[dst:dst + rows, :] = a
                u_ref[dst:dst + rows, :] = mult * (t_i * xh + xh)

        def scan_body(j, carry):
            hf, pf, hb, pb = carry
            fi = pl.ds(j, SUBLANES, stride=pitch)
            ri = pl.ds(lc - 1 - j, SUBLANES, stride=pitch)
            a_f = af[fi, :]
            a_b = ab[ri, :]
            hf = a_f * hf + uf[fi, :]
            hb = a_b * hb + ub[ri, :]
            pf = a_f * pf
            pb = a_b * pb
            uf[fi, :] = hf
            af[fi, :] = pf
            ub[ri, :] = hb
            ab[ri, :] = pb
            return hf, pf, hb, pb

        zeros = jnp.zeros((SUBLANES, LANES), F32)
        ones = jnp.ones((SUBLANES, LANES), F32)
        hf_end, af_end, hb_end, ab_end = lax.fori_loop(0, lc, scan_body, (zeros, ones, zeros, ones), unroll=8)

        h0 = h0_ref[bi]
        cf = [h0[0:1, :]]
        for s in range(SUBLANES):
            cf.append(af_end[s:s + 1, :] * cf[s] + hf_end[s:s + 1, :])
        cb = [None] * (SUBLANES + 1)
        cb[SUBLANES] = h0[1:2, :]
        for s in range(SUBLANES - 1, -1, -1):
            cb[s] = ab_end[s:s + 1, :] * cb[s + 1] + hb_end[s:s + 1, :]
        st_ref[bi] = jnp.concatenate([cf[SUBLANES], cb[0]], axis=0)

        for s in range(SUBLANES):
            src = slice(s * pitch, s * pitch + lc)
            rs = slice(s * lc, (s + 1) * lc)
            hf = uf[src, :] + af[src, :] * cf[s]
            hb = ub[src, :] + ab[src, :] * cb[s + 1]
            y_ref[bi, rs, :] = (gg_ref[bi, rs, :] * (hf + hb)).astype(BF16)

    if n_seq == 1:
        one_sequence(0)
    else:
        def seq_body(bi, carry):
            one_sequence(bi)
            return carry
        lax.fori_loop(0, n_seq, seq_body, 0)


def _lru_call(xc, gc, lw, h0):
    nb, n, _ = xc.shape
    ng = C_WIDTH // LANES
    lc = n // SUBLANES
    rows = min(lc, 512)
    n_scan = SUBLANES * _chunk_pitch(lc)
    bb = _pick_tile(nb, max(1, 1024 // n))
    seq = lambda b, g: (b, 0, g)
    return pl.pallas_call(
        functools.partial(_lru_kernel, seq_len=n, rows=rows, n_seq=bb),
        grid=(nb // bb, ng),
        in_specs=[
            pl.BlockSpec((bb, n, LANES), seq),
            pl.BlockSpec((bb, n, LANES), seq),
            pl.BlockSpec((C_CONV, LANES), lambda b, g: (0, g)),
            pl.BlockSpec((1, LANES), lambda b, g: (0, g)),
            pl.BlockSpec((2, LANES), lambda b, g: (0, g)),
            pl.BlockSpec((1, LANES, 4 * LANES), lambda b, g: (g, 0, 0)),
            pl.BlockSpec((1, 1, 4 * LANES), lambda b, g: (g, 0, 0)),
            pl.BlockSpec((bb, 2, LANES), seq),
        ],
        out_specs=[pl.BlockSpec((bb, n, LANES), seq), pl.BlockSpec((bb, 2, LANES), seq)],
        out_shape=[jax.ShapeDtypeStruct((nb, n, C_WIDTH), BF16),
                   jax.ShapeDtypeStruct((nb, 2, C_WIDTH), F32)],
        scratch_shapes=[pltpu.VMEM((n + 2 * SUBLANES, LANES), F32)] + [pltpu.VMEM((n_scan, LANES), F32)] * 4,
        compiler_params=_cparams("arbitrary", "arbitrary"),
        name="rglru",
    )(xc, gc, lw["w_conv"], lw["b_conv"], lw["lam"], lw["wgate"], lw["bgate"], h0)


def _merge_kernel(x_ref, ya_ref, yb_ref, yc_ref, gl_ref, g1_ref, gpost_ref, wbr_ref, wo_ref, o_ref):
    m = None
    for nbr, y_ref in enumerate((ya_ref, yb_ref, yc_ref)):
        gate = jax.nn.sigmoid(gl_ref[:, nbr * D_MODEL:(nbr + 1) * D_MODEL].astype(F32))
        term = gate * _dot(y_ref[...], wbr_ref[nbr])
        m = term if m is None else m + term
    o = _dot(m.astype(BF16), wo_ref[...])
    o_ref[...] = x_ref[...] + g1_ref[0] * _rms(o, gpost_ref[...])


def _merge_call(x, ya, yb, yc, gl, mod3, mod_row, lw, *, tm):
    t = x.shape[0]
    row = lambda i: (i, 0)
    return pl.pallas_call(
        _merge_kernel,
        grid=(t // tm,),
        in_specs=[
            pl.BlockSpec((tm, D_MODEL), row),
            pl.BlockSpec((tm, MIX_W), row),
            pl.BlockSpec((tm, MIX_W), row),
            pl.BlockSpec((tm, MIX_W), row),
            pl.BlockSpec((tm, N_BRANCH * D_MODEL), row),
            pl.BlockSpec((1, 1, D_MODEL), lambda i: (mod_row(i), 0, 2)),
            pl.BlockSpec((1, D_MODEL), lambda i: (0, 0)),
            pl.BlockSpec((N_BRANCH, MIX_W, D_MODEL), lambda i: (0, 0, 0)),
            pl.BlockSpec((D_MODEL, D_MODEL), lambda i: (0, 0)),
        ],
        out_specs=pl.BlockSpec((tm, D_MODEL), row),
        out_shape=jax.ShapeDtypeStruct((t, D_MODEL), F32),
        compiler_params=_cparams("arbitrary"),
        name="merge",
    )(x, ya, yb, yc, gl, mod3, lw["g_post1"], lw["w_branch"], lw["w_out"])


def _mlp_kernel(x_ref, sh_ref, sc_ref, g2_ref, gpre_ref, gpost_ref, w1_ref, w2_ref, o_ref, *, ff_chunk):
    x = x_ref[...]
    hb = (_rms(x, gpre_ref[...]) * (1.0 + sc_ref[0]) + sh_ref[0]).astype(BF16)
    f = None
    for c0 in range(0, D_FF, ff_chunk):
        u = jnp.maximum(_dot(hb, w1_ref[:, c0:c0 + ff_chunk]), 0.0)
        part = _dot((u * u).astype(BF16), w2_ref[c0:c0 + ff_chunk, :])
        f = part if f is None else f + part
    o_ref[...] = x + g2_ref[0] * _rms(f, gpost_ref[...])


def _mlp_call(x, mod3, mod_row, lw, *, tm):
    t = x.shape[0]
    row = lambda i: (i, 0)
    modspec = lambda col: pl.BlockSpec((1, 1, D_MODEL), lambda i: (mod_row(i), 0, col))
    return pl.pallas_call(
        functools.partial(_mlp_kernel, ff_chunk=1024),
        grid=(t // tm,),
        in_specs=[
            pl.BlockSpec((tm, D_MODEL), row),
            modspec(3), modspec(4), modspec(5),
            pl.BlockSpec((1, D_MODEL), lambda i: (0, 0)),
            pl.BlockSpec((1, D_MODEL), lambda i: (0, 0)),
            pl.BlockSpec((D_MODEL, D_FF), lambda i: (0, 0)),
            pl.BlockSpec((D_FF, D_MODEL), lambda i: (0, 0)),
        ],
        out_specs=pl.BlockSpec((tm, D_MODEL), row),
        out_shape=jax.ShapeDtypeStruct((t, D_MODEL), F32),
        compiler_params=_cparams("arbitrary"),
        name="mlp",
    )(x, mod3, mod3, mod3, lw["g_pre2"], lw["g_post2"], lw["w_ff1"], lw["w_ff2"])


def _merge_mlp_kernel(x_ref, ya_ref, yb_ref, yc_ref, gl_ref, g1_ref, gpost1_ref, wbr_ref, wo_ref,
                      sh_ref, sc_ref, g2_ref, gpre2_ref, gpost2_ref, w1_ref, w2_ref, o_ref, *, ff_chunk):
    m = None
    for nbr, y_ref in enumerate((ya_ref, yb_ref, yc_ref)):
        gate = jax.nn.sigmoid(gl_ref[:, nbr * D_MODEL:(nbr + 1) * D_MODEL].astype(F32))
        term = gate * _dot(y_ref[...], wbr_ref[nbr])
        m = term if m is None else m + term
    o = _dot(m.astype(BF16), wo_ref[...])
    x = x_ref[...] + g1_ref[0] * _rms(o, gpost1_ref[...])
    hb = (_rms(x, gpre2_ref[...]) * (1.0 + sc_ref[0]) + sh_ref[0]).astype(BF16)
    f = None
    for c0 in range(0, D_FF, ff_chunk):
        u = jnp.maximum(_dot(hb, w1_ref[:, c0:c0 + ff_chunk]), 0.0)
        part = _dot((u * u).astype(BF16), w2_ref[c0:c0 + ff_chunk, :])
        f = part if f is None else f + part
    o_ref[...] = x + g2_ref[0] * _rms(f, gpost2_ref[...])


def _merge_mlp_call(x, ya, yb, yc, gl, mod3, mod_row, lw, *, tm):
    t = x.shape[0]
    row = lambda i: (i, 0)
    modspec = lambda col: pl.BlockSpec((1, 1, D_MODEL), lambda i: (mod_row(i), 0, col))
    once = lambda shape: pl.BlockSpec(shape, lambda i: (0,) * len(shape), pipeline_mode=pl.Buffered(1))
    return pl.pallas_call(
        functools.partial(_merge_mlp_kernel, ff_chunk=1024),
        grid=(t // tm,),
        in_specs=[
            pl.BlockSpec((tm, D_MODEL), row),
            pl.BlockSpec((tm, MIX_W), row),
            pl.BlockSpec((tm, MIX_W), row),
            pl.BlockSpec((tm, MIX_W), row),
            pl.BlockSpec((tm, N_BRANCH * D_MODEL), row),
            modspec(2),
            once((1, D_MODEL)),
            once((N_BRANCH, MIX_W, D_MODEL)),
            once((D_MODEL, D_MODEL)),
            modspec(3), modspec(4), modspec(5),
            once((1, D_MODEL)),
            once((1, D_MODEL)),
            once((D_MODEL, D_FF)),
            once((D_FF, D_MODEL)),
        ],
        out_specs=pl.BlockSpec((tm, D_MODEL), row),
        out_shape=jax.ShapeDtypeStruct((t, D_MODEL), F32),
        compiler_params=_cparams("arbitrary"),
        name="merge_mlp",
    )(x, ya, yb, yc, gl, mod3, lw["g_post1"], lw["w_branch"], lw["w_out"],
      mod3, mod3, mod3, lw["g_pre2"], lw["g_post2"], lw["w_ff1"], lw["w_ff2"])


def _rope_tables(n):
    pos = jnp.arange(n)
    rows = (pos // GRID_W).astype(F32)[:, None]
    cols = (pos % GRID_W).astype(F32)[:, None]

    def pattern(dim):
        half = dim // 4
        inv = jnp.power(ROPE_BASE, -jnp.arange(half, dtype=F32) * (2.0 / (dim // 2)))
        ar, ac = rows * inv, cols * inv
        c = jnp.concatenate([jnp.cos(ar), jnp.cos(ar), jnp.cos(ac), jnp.cos(ac)], axis=1)
        s = jnp.concatenate([-jnp.sin(ar), jnp.sin(ar), -jnp.sin(ac), jnp.sin(ac)], axis=1)
        return c, s

    c64, s64 = pattern(A_HEAD_DIM)
    ca = jnp.tile(c64, (1, LANES // A_HEAD_DIM))
    sa = jnp.tile(s64, (1, LANES // A_HEAD_DIM))
    c32, s32 = pattern(B_ROPE)
    tail = HEAD_PAD - B_QK_DIM
    cb = jnp.concatenate([jnp.ones((n, B_NOPE), F32), c32, jnp.ones((n, tail), F32)], axis=1)
    sb = jnp.concatenate([jnp.zeros((n, B_NOPE), F32), s32, jnp.zeros((n, tail), F32)], axis=1)
    return (ca, sa, cb, sb, (ca * A_QSCALE).T, (sa * A_QSCALE).T, (cb * B_QSCALE).T, (sb * B_QSCALE).T)


def _prep_weights(p):
    w_in = p["w_in"]
    qw = A_HEADS * A_HEAD_DIM
    o_k = qw
    o_v = o_k + LANES
    o_cq = o_v + LANES
    o_ckv = o_cq + B_Q_LORA
    o_kr = o_ckv + B_KV_LORA
    o_xc = o_kr + B_ROPE
    o_gl = o_xc + 2 * C_WIDTH
    w_q, w_k, w_v = w_in[:, :, :o_k], w_in[:, :, o_k:o_v], w_in[:, :, o_v:o_cq]
    w_cq, w_ckv, w_kr = w_in[:, :, o_cq:o_ckv], w_in[:, :, o_ckv:o_kr], w_in[:, :, o_kr:o_xc]
    kr_placed = jnp.pad(w_kr, ((0, 0), (0, 0), (B_NOPE, HEAD_PAD - B_QK_DIM)))
    pad_heads = lambda w, dh: jnp.pad(
        w.reshape(DEPTH, w.shape[1], B_HEADS, dh), ((0, 0), (0, 0), (0, 0), (0, HEAD_PAD - dh))
    ).reshape(DEPTH, w.shape[1], B_HEADS * HEAD_PAD)
    tr = lambda w: jnp.swapaxes(w, 1, 2)

    eye = jnp.eye(C_BLOCKS, dtype=F32)
    ng = C_WIDTH // LANES

    def gate_groups(w):
        full = (w[:, :, :, None, :] * eye[None, :, None, :, None]).reshape(DEPTH, C_WIDTH, C_WIDTH)
        full = full.reshape(DEPTH, ng, LANES, ng, LANES)
        return jnp.stack([full[:, g, :, g, :] for g in range(ng)], axis=1)

    w_ra, w_ri, b_ra, b_ri = p["w_ra"], p["w_ri"], p["b_ra"], p["b_ri"]
    wgate = jnp.concatenate([gate_groups(w_ra[:, 0]), gate_groups(w_ri[:, 0]),
                             gate_groups(w_ra[:, 1]), gate_groups(w_ri[:, 1])], axis=3)
    grp = lambda b: b.reshape(DEPTH, ng, 1, LANES)
    bgate = jnp.concatenate([grp(b_ra[:, 0]), grp(b_ri[:, 0]), grp(b_ra[:, 1]), grp(b_ri[:, 1])], axis=3)

    row = lambda g: g[:, None, :]
    return dict(
        g_pre1=row(p["g_pre1"]), g_post1=row(p["g_post1"]), g_pre2=row(p["g_pre2"]), g_post2=row(p["g_post2"]),
        g_q=row(p["g_q"]), g_kv=row(p["g_kv"]),
        wqat=tr(w_q).astype(BF16),
        wkv=jnp.concatenate([w_k, w_v], axis=2).astype(BF16),
        wvat=tr(w_v).astype(BF16),
        wb=jnp.concatenate([w_cq, kr_placed, w_ckv], axis=2).astype(BF16),
        wl=w_in[:, :, o_xc:o_gl].astype(BF16), wg=w_in[:, :, o_gl:].astype(BF16),
        wqt=tr(pad_heads(p["w_q_up"], B_QK_DIM)).astype(BF16),
        wuk=pad_heads(p["w_uk"], B_NOPE).astype(BF16),
        wuvt=tr(p["w_uv"]).astype(BF16),
        w_conv=p["w_conv"], b_conv=row(p["b_conv"]), lam=p["lam"],
        wgate=(0.5 * wgate).astype(BF16), bgate=0.5 * bgate,
        w_branch=p["w_branch"].astype(BF16), w_out=p["w_out"].astype(BF16),
        w_ff1=p["w_ff1"].astype(BF16), w_ff2=p["w_ff2"].astype(BF16),
    )


def _pick_tile(n, target, multiple=1):
    t = max(multiple, min(n, target) // multiple * multiple)
    while n % t:
        t -= multiple
    return t


def kernel(x_prompt, x_sample, cache_attn_k, cache_attn_v, cache_mla_ckv, cache_mla_krope, state_lru, c, c_ctx, w_ada, b_ada, g_pre1, g_post1, g_pre2, g_post2, w_in, sink, g_q, w_q_up, g_kv, w_uk, w_uv, w_conv, b_conv, lam, w_ra, b_ra, w_ri, b_ri, w_branch, w_out, w_ff1, w_ff2):
    nbc, seq, _ = x_prompt.shape
    nbl, n_lat, _ = x_sample.shape
    past = cache_attn_k.shape[2]
    assert n_lat % GRID_W == 0 and n_lat >= GQA_Q_TILE + 2 * A_WINDOW and n_lat % GQA_Q_TILE == 0
    assert seq % (SUBLANES * SUBLANES) == 0 and n_lat % (SUBLANES * SUBLANES) == 0

    weights = _prep_weights(dict(
        g_pre1=g_pre1, g_post1=g_post1, g_pre2=g_pre2, g_post2=g_post2, w_in=w_in, g_q=g_q,
        w_q_up=w_q_up, g_kv=g_kv, w_uk=w_uk, w_uv=w_uv, w_conv=w_conv, b_conv=b_conv, lam=lam,
        w_ra=w_ra, b_ra=b_ra, w_ri=w_ri, b_ri=b_ri, w_branch=w_branch, w_out=w_out,
        w_ff1=w_ff1, w_ff2=w_ff2))
    tables = _rope_tables(n_lat)

    n_mod = -(-(nbl + 1) // SUBLANES) * SUBLANES
    cmat = jnp.zeros((n_mod, D_MODEL), F32).at[:nbl].set(c).at[nbl].set(c_ctx)
    mod3 = _ada_call(cmat, w_ada, b_ada).reshape(DEPTH * n_mod, 1, N_ADA * D_MODEL)

    place = jnp.pad(jnp.eye(B_ROPE, dtype=BF16), ((0, 0), (B_NOPE, HEAD_PAD - B_QK_DIM)))
    kc_mla, vtc_mla = _mla_cache_call(cache_mla_ckv, cache_mla_krope, weights["wuk"], weights["wuvt"], place)
    kc_a = jnp.swapaxes(cache_attn_k, 2, 3).astype(BF16)
    ones_rows = jnp.zeros((nbl, DEPTH, A_KV_HEADS, V_ROWS - A_HEAD_DIM, past), BF16).at[:, :, :, 0].set(1)
    vtc_a = jnp.concatenate(
        [jnp.transpose(cache_attn_v, (0, 1, 3, 4, 2)).astype(BF16), ones_rows], axis=3)

    tm_p = _pick_tile(seq, 512)
    tm_pw = _pick_tile(nbc * seq, 512, SUBLANES)
    tm_s = _pick_tile(n_lat, 512)
    y_p = x_prompt.reshape(nbc * seq, D_MODEL)
    y_s = x_sample.reshape(nbl * n_lat, D_MODEL)
    zero_state = jnp.zeros((nbc, 2, C_WIDTH), F32)
    ks_l, vs_l, ckv_l, kr_l, st_l = [], [], [], [], []
    for l in range(DEPTH):
        lw = {k: v[l] for k, v in weights.items()}
        row_p = lambda i, l=l: l * n_mod + nbl
        row_s = lambda i, l=l: l * n_mod + (i * tm_s) // n_lat

        (qat, ka, vat, qbt, kb, vbt, xc, gc, gl, kaf, vaf, ckvn, krf) = _inproj_call(
            y_p, mod3, row_p, lw, None, latent=False, seq_len=seq, tm=tm_p)
        r3 = lambda a: a.reshape(nbc, seq, a.shape[-1])
        ya = _gqa_call(sink[l], qat, ka, vat)
        yb = _mla_call(qbt, r3(kb), vbt, tq=seq, tk=seq)
        yc, st = _lru_call(r3(xc), r3(gc), lw, zero_state)
        f2 = lambda a: a.reshape(nbc * seq, a.shape[-1])
        y_p = _merge_mlp_call(y_p, f2(ya), f2(yb), f2(yc), gl, mod3, row_p, lw, tm=tm_pw)
        ks_l.append(kaf.reshape(nbc, seq, A_KV_HEADS, A_HEAD_DIM))
        vs_l.append(vaf.reshape(nbc, seq, A_KV_HEADS, A_HEAD_DIM))
        ckv_l.append(ckvn.reshape(nbc, seq, B_KV_LORA))
        kr_l.append(krf.reshape(nbc, seq, B_ROPE))
        st_l.append(st)

        (qat, ka, vat, qbt, kb, vbt, xc, gc, gl) = _inproj_call(
            y_s, mod3, row_s, lw, tables, latent=True, seq_len=n_lat, tm=tm_s)
        r3 = lambda a: a.reshape(nbl, n_lat, a.shape[-1])
        ya = _gqa_call(sink[l], qat, ka, vat, kc_a, vtc_a, layer=l)
        yb = _mla_call(qbt, r3(kb), vbt, kc_mla, vtc_mla, layer=l, tq=_pick_tile(n_lat, 512, LANES),
                       tk=_pick_tile(math.gcd(n_lat, past), 256, LANES))
        yc, _ = _lru_call(r3(xc), r3(gc), lw, state_lru[:, l])
        f2 = lambda a: a.reshape(nbl * n_lat, a.shape[-1])
        y_s = _merge_mlp_call(y_s, f2(ya), f2(yb), f2(yc), gl, mod3, row_s, lw, tm=tm_s)

    return (y_p.reshape(nbc, seq, D_MODEL), y_s.reshape(nbl, n_lat, D_MODEL),
            jnp.stack(ks_l, axis=1), jnp.stack(vs_l, axis=1), jnp.stack(ckv_l, axis=1),
            jnp.stack(kr_l, axis=1), jnp.stack(st_l, axis=1))
```
